```python
import math
import jax
import jax.numpy as jnp
from jax import lax
import numpy as np

D_MODEL = 1024
BATCH = 32
SEQ = 2048
DEPTH = 1

HEAD_DIM = 64
A_HEADS = 8
A_KV = 2
A_REP = A_HEADS // A_KV
A_WINDOW = 128
B_HEADS = 8
B_KV = 2
B_REP = B_HEADS // B_KV
CMP_LEN = 32
CMP_STRIDE = 16
CMP_HIDDEN = 128
SLC_LEN = 64
SLC_TOP = 8
SLC_LOCAL = 2
B_WINDOW = 512
N_BUCKETS = 32
REL_MAX_DIST = 128
N_EXPERTS = 32
TOP_K = 4
D_FF = D_MODEL
SWIGLU_LIMIT = 7.0
SWIGLU_ALPHA = 1.702
Q_BLOCK = 128
SLC_Q_BLOCK = 64
MOE_BLOCK = 256
LN_EPS = 1e-5
NEG_INF = -1e30
FORCED_SCORE = 1e30
DEEPNORM_ALPHA = (2 * DEPTH) ** 0.25
DEEPNORM_BETA = (8 * DEPTH) ** -0.25
A_WIDTH = A_HEADS * HEAD_DIM
B_WIDTH = B_HEADS * HEAD_DIM
A_KV_WIDTH = A_KV * HEAD_DIM
B_KV_WIDTH = B_KV * HEAD_DIM
IN_WIDTHS = (A_WIDTH, A_KV_WIDTH, A_KV_WIDTH, B_WIDTH, B_KV_WIDTH, B_KV_WIDTH, B_KV_WIDTH, B_KV_WIDTH, B_KV_WIDTH, B_KV_WIDTH, 3 * B_HEADS, 2 * D_MODEL)
IN_WIDTH = sum(IN_WIDTHS)

kernel_name = 'hybrid_swa_nsa_moe_block'


def layer_norm(x, g, b):
    xf = x.astype(jnp.float32)
    mu = jnp.mean(xf, axis=-1, keepdims=True)
    var = jnp.mean(jnp.square(xf - mu), axis=-1, keepdims=True)
    return ((xf - mu) * lax.rsqrt(var + LN_EPS)).astype(x.dtype) * g + b


def t5_bucket(rel):
    n = jnp.maximum(rel, 0)
    max_exact = N_BUCKETS // 2
    nf = jnp.maximum(n, 1).astype(jnp.float32)
    large = max_exact + (jnp.log(nf / max_exact) / math.log(REL_MAX_DIST / max_exact) * (N_BUCKETS - max_exact)).astype(jnp.int32)
    large = jnp.minimum(large, N_BUCKETS - 1)
    return jnp.where(n < max_exact, n, large)


def banded_attention(q, k, v, bias_table, window, sinks=None):
    bsz, seq, g, r, d = q.shape
    n_prev = -(-window // Q_BLOCK)
    kb = (n_prev + 1) * Q_BLOCK
    nb = seq // Q_BLOCK
    pad = ((0, 0), (n_prev * Q_BLOCK, 0), (0, 0), (0, 0))
    kp = jnp.pad(k, pad)
    vp = jnp.pad(v, pad)
    q_off = np.arange(Q_BLOCK)[:, None]
    k_off = np.arange(kb)[None, :]
    rel = n_prev * Q_BLOCK + q_off - k_off
    band = (rel >= 0) & (rel < window)
    bias = jnp.transpose(bias_table[t5_bucket(jnp.asarray(rel))], (2, 3, 0, 1))
    scale = d ** -0.5

    def one_block(i):
        qb = lax.dynamic_slice_in_dim(q, i * Q_BLOCK, Q_BLOCK, axis=1)
        kblk = lax.dynamic_slice_in_dim(kp, i * Q_BLOCK, kb, axis=1)
        vblk = lax.dynamic_slice_in_dim(vp, i * Q_BLOCK, kb, axis=1)
        s = jnp.einsum('bqgrd,bkgd->bgrqk', qb, kblk).astype(jnp.float32) * scale + bias
        kpos = i * Q_BLOCK - n_prev * Q_BLOCK + jnp.asarray(k_off)
        s = jnp.where(band & (kpos >= 0), s, NEG_INF)
        if sinks is None:
            p = jax.nn.softmax(s, axis=-1)
        else:
            sk = sinks.astype(jnp.float32)[None, :, :, None, None]
            m = jnp.maximum(jnp.max(s, axis=-1, keepdims=True), sk)
            e = jnp.exp(s - m)
            p = e / (jnp.sum(e, axis=-1, keepdims=True) + jnp.exp(sk - m))
        return jnp.einsum('bgrqk,bkgd->bqgrd', p.astype(v.dtype), vblk)

    out = lax.map(one_block, jnp.arange(nb))
    return jnp.moveaxis(out, 0, 1).reshape(bsz, seq, g, r, d)


def compress_blocks(kv, pos, w1, w2):
    bsz, seq, g, d = kv.shape
    nc = (seq - CMP_LEN) // CMP_STRIDE + 1
    idx = np.arange(nc)[:, None] * CMP_STRIDE + np.arange(CMP_LEN)[None, :]
    blocks = kv[:, idx] + pos[None, None, :, None, :]
    blocks = jnp.transpose(blocks, (0, 1, 3, 2, 4)).reshape(bsz, nc, g, CMP_LEN * d)
    return jax.nn.gelu(blocks @ w1) @ w2


def cmp_to_slc_matrix(nc, nslc):
    cs = np.arange(nc)[:, None] * CMP_STRIDE
    ss = np.arange(nslc)[None, :] * SLC_LEN
    ov = np.clip(np.minimum(cs + CMP_LEN, ss + SLC_LEN) - np.maximum(cs, ss), 0, None)
    return (ov / CMP_LEN).astype(np.float32)


def selected_attention(q, k, v, sel, bias_table):
    bsz, seq, g, r, d = q.shape
    nslc = seq // SLC_LEN
    top = sel.shape[-1]
    kt = jnp.transpose(k.reshape(bsz, nslc, SLC_LEN, g, d), (0, 3, 1, 2, 4))
    vt = jnp.transpose(v.reshape(bsz, nslc, SLC_LEN, g, d), (0, 3, 1, 2, 4))
    b_idx = jnp.arange(bsz)[:, None, None, None]
    g_idx = jnp.arange(g)[None, None, :, None]
    g_idx5 = jnp.arange(g)[None, None, :, None, None]
    scale = d ** -0.5

    def one_chunk(i):
        qc = lax.dynamic_slice_in_dim(q, i * SLC_Q_BLOCK, SLC_Q_BLOCK, axis=1)
        sc = lax.dynamic_slice_in_dim(sel, i * SLC_Q_BLOCK, SLC_Q_BLOCK, axis=1)
        kg = kt[b_idx, g_idx, sc]
        vg = vt[b_idx, g_idx, sc]
        kpos = sc[..., None] * SLC_LEN + jnp.arange(SLC_LEN)
        tpos = i * SLC_Q_BLOCK + jnp.arange(SLC_Q_BLOCK)
        rel = tpos[None, :, None, None, None] - kpos
        bias = jnp.moveaxis(bias_table[t5_bucket(rel), g_idx5], -1, 3)
        s = jnp.einsum('bqgrd,bqgnld->bqgrnl', qc, kg).astype(jnp.float32) * scale + bias
        s = jnp.where((rel >= 0)[:, :, :, None], s, NEG_INF)
        p = jax.nn.softmax(s.reshape(s.shape[:4] + (top * SLC_LEN,)), axis=-1).reshape(s.shape)
        return jnp.einsum('bqgrnl,bqgnld->bqgrd', p.astype(v.dtype), vg)

    out = lax.map(one_chunk, jnp.arange(seq // SLC_Q_BLOCK))
    return jnp.moveaxis(out, 0, 1).reshape(bsz, seq, g, r, d)


def nsa_attention(q, k_cmp, v_cmp, k_slc, v_slc, k_win, v_win, gate_logits, bias_table,
                  cmp_pos_k, cmp_w1_k, cmp_w2_k, cmp_pos_v, cmp_w1_v, cmp_w2_v):
    bsz, seq, g, r, d = q.shape
    scale = d ** -0.5
    t = np.arange(seq)
    kc = compress_blocks(k_cmp, cmp_pos_k, cmp_w1_k, cmp_w2_k)
    vc = compress_blocks(v_cmp, cmp_pos_v, cmp_w1_v, cmp_w2_v)
    nc = kc.shape[1]
    cmp_valid = (np.arange(nc) * CMP_STRIDE + CMP_LEN - 1)[None, :] <= t[:, None]
    s = jnp.einsum('bsgrd,bcgd->bgrsc', q, kc).astype(jnp.float32) * scale
    s = jnp.where(cmp_valid, s, NEG_INF)
    p_cmp = jnp.where(cmp_valid.any(axis=-1)[:, None], jax.nn.softmax(s, axis=-1), 0.0)
    o_cmp = jnp.einsum('bgrsc,bcgd->bsgrd', p_cmp.astype(vc.dtype), vc)
    nslc = seq // SLC_LEN
    imp = jnp.einsum('bgrsc,cj->bsgj', p_cmp, cmp_to_slc_matrix(nc, nslc))
    blk = np.arange(nslc)[None, :]
    cur = (t // SLC_LEN)[:, None]
    forced = (blk == 0) | ((blk <= cur) & (blk > cur - SLC_LOCAL))
    future = blk > cur
    imp = jnp.where(forced[:, None, :], FORCED_SCORE, imp)
    imp = jnp.where(future[:, None, :], NEG_INF, imp)
    _, sel = lax.top_k(imp, min(SLC_TOP, nslc))
    o_slc = selected_attention(q, k_slc, v_slc, sel, bias_table)
    o_win = banded_attention(q, k_win, v_win, bias_table, B_WINDOW)
    gts = jax.nn.sigmoid(gate_logits)
    return gts[..., 0:1] * o_cmp + gts[..., 1:2] * o_slc + gts[..., 2:3] * o_win


def hybrid_mixer(x, w_in, b_in, rel_bias, attn_sinks, cmp_pos_k, cmp_w1_k, cmp_w2_k,
                 cmp_pos_v, cmp_w1_v, cmp_w2_v, w_branch_a, w_branch_b, w_out):
    bsz, seq, _ = x.shape
    proj = x @ w_in + b_in
    splits = []
    off = 0
    for w in IN_WIDTHS[:-1]:
        off += w
        splits.append(off)
    (qa, ka, va, qb, kbc, vbc, kbs, vbs, kbw, vbw, nsa_g, merge_g) = jnp.split(proj, splits, axis=-1)

    def heads(z, g, r):
        return z.reshape(bsz, seq, g, r, HEAD_DIM)

    def kvh(z, g):
        return z.reshape(bsz, seq, g, HEAD_DIM)

    bias_a = rel_bias[:, :A_HEADS].reshape(N_BUCKETS, A_KV, A_REP)
    bias_b = rel_bias[:, A_HEADS:].reshape(N_BUCKETS, B_KV, B_REP)
    y_a = banded_attention(heads(qa, A_KV, A_REP), kvh(ka, A_KV), kvh(va, A_KV), bias_a,
                           A_WINDOW, attn_sinks.reshape(A_KV, A_REP))
    y_b = nsa_attention(heads(qb, B_KV, B_REP), kvh(kbc, B_KV), kvh(vbc, B_KV), kvh(kbs, B_KV),
                        kvh(vbs, B_KV), kvh(kbw, B_KV), kvh(vbw, B_KV),
                        nsa_g.reshape(bsz, seq, B_KV, B_REP, 3), bias_b,
                        cmp_pos_k, cmp_w1_k, cmp_w2_k, cmp_pos_v, cmp_w1_v, cmp_w2_v)
    gate_a, gate_b = jnp.split(jax.nn.sigmoid(merge_g), 2, axis=-1)
    merged = (gate_a * (y_a.reshape(bsz, seq, A_WIDTH) @ w_branch_a)
              + gate_b * (y_b.reshape(bsz, seq, B_WIDTH) @ w_branch_b))
    return merged @ w_out


def moe_ffn(x2d, w_router, b_router, w_gate_up, b_gate_up, w_down, b_down):
    n_tok, d = x2d.shape
    logits = (x2d @ w_router + b_router).astype(jnp.float32)
    top_vals, top_idx = lax.top_k(logits, TOP_K)
    gates = jax.nn.softmax(top_vals, axis=-1)
    flat_e = top_idx.reshape(-1)
    n_assign = flat_e.shape[0]
    order = jnp.argsort(flat_e)
    sorted_e = flat_e[order]
    tok = order // TOP_K
    counts = jnp.bincount(flat_e, length=N_EXPERTS)
    padded = (counts + MOE_BLOCK - 1) // MOE_BLOCK * MOE_BLOCK
    start = jnp.cumsum(counts) - counts
    padded_end = jnp.cumsum(padded)
    padded_start = padded_end - padded
    dest = padded_start[sorted_e] + jnp.arange(n_assign) - start[sorted_e]
    n_blocks = -(-n_assign // MOE_BLOCK) + N_EXPERTS
    rows = jnp.zeros((n_blocks * MOE_BLOCK, d), x2d.dtype).at[dest].set(x2d[tok])
    block_expert = jnp.minimum(jnp.searchsorted(padded_end, jnp.arange(n_blocks) * MOE_BLOCK, side='right'), N_EXPERTS - 1)

    def expert_block(args):
        xb, e = args
        h = xb @ w_gate_up[e] + b_gate_up[e]
        glu = jnp.minimum(h[:, :D_FF], SWIGLU_LIMIT)
        lin = jnp.clip(h[:, D_FF:], -SWIGLU_LIMIT, SWIGLU_LIMIT)
        act = glu * jax.nn.sigmoid(SWIGLU_ALPHA * glu) * (lin + 1.0)
        return act @ w_down[e] + b_down[e]

    out_rows = lax.map(expert_block, (rows.reshape(n_blocks, MOE_BLOCK, d), block_expert))
    out_rows = out_rows.reshape(n_blocks * MOE_BLOCK, d)
    weights = gates.reshape(-1)[order].astype(x2d.dtype)
    return jnp.zeros_like(x2d).at[tok].add(out_rows[dest] * weights[:, None])


def setup_inputs(seed: int = 0) -> dict:
    key = jax.random.key(seed)
    ks = jax.random.split(key, 24)

    def nrm(k, shape, scale):
        return jax.random.normal(k, shape, jnp.float32) * scale

    L = DEPTH
    fan_cmp = CMP_LEN * HEAD_DIM
    return {
        'x': nrm(ks[0], (BATCH, SEQ, D_MODEL), 1.0),
        'w_in': nrm(ks[1], (L, D_MODEL, IN_WIDTH), D_MODEL ** -0.5),
        'b_in': nrm(ks[2], (L, IN_WIDTH), 0.02),
        'rel_bias': nrm(ks[3], (N_BUCKETS, A_HEADS + B_HEADS), 0.3),
        'attn_sinks': nrm(ks[4], (L, A_HEADS), 0.5),
        'cmp_pos_k': nrm(ks[5], (L, CMP_LEN, HEAD_DIM), 0.1),
        'cmp_w1_k': nrm(ks[6], (L, fan_cmp, CMP_HIDDEN), fan_cmp ** -0.5),
        'cmp_w2_k': nrm(ks[7], (L, CMP_HIDDEN, HEAD_DIM), CMP_HIDDEN ** -0.5),
        'cmp_pos_v': nrm(ks[8], (L, CMP_LEN, HEAD_DIM), 0.1),
        'cmp_w1_v': nrm(ks[9], (L, fan_cmp, CMP_HIDDEN), fan_cmp ** -0.5),
        'cmp_w2_v': nrm(ks[10], (L, CMP_HIDDEN, HEAD_DIM), CMP_HIDDEN ** -0.5),
        'w_branch_a': nrm(ks[11], (L, A_WIDTH, D_MODEL), A_WIDTH ** -0.5),
        'w_branch_b': nrm(ks[12], (L, B_WIDTH, D_MODEL), B_WIDTH ** -0.5),
        'w_out': nrm(ks[13], (L, D_MODEL, D_MODEL), D_MODEL ** -0.5 * DEEPNORM_BETA),
        'ln1_g': 1.0 + nrm(ks[14], (L, D_MODEL), 0.05),
        'ln1_b': nrm(ks[15], (L, D_MODEL), 0.02),
        'w_router': nrm(ks[16], (L, D_MODEL, N_EXPERTS), D_MODEL ** -0.5),
        'b_router': nrm(ks[17], (L, N_EXPERTS), 0.01),
        'w_gate_up': nrm(ks[18], (L, N_EXPERTS, D_MODEL, 2 * D_FF), D_MODEL ** -0.5),
        'b_gate_up': nrm(ks[19], (L, N_EXPERTS, 2 * D_FF), 0.02),
        'w_down': nrm(ks[20], (L, N_EXPERTS, D_FF, D_MODEL), D_FF ** -0.5 * DEEPNORM_BETA),
        'b_down': nrm(ks[21], (L, N_EXPERTS, D_MODEL), 0.02),
        'ln2_g': 1.0 + nrm(ks[22], (L, D_MODEL), 0.05),
        'ln2_b': nrm(ks[23], (L, D_MODEL), 0.02),
    }


def reference(x, w_in, b_in, rel_bias, attn_sinks, cmp_pos_k, cmp_w1_k, cmp_w2_k,
              cmp_pos_v, cmp_w1_v, cmp_w2_v, w_branch_a, w_branch_b, w_out, ln1_g, ln1_b,
              w_router, b_router, w_gate_up, b_gate_up, w_down, b_down, ln2_g, ln2_b):
    bsz, seq, d = x.shape
    h = x
    for l in range(DEPTH):
        mix = hybrid_mixer(h, w_in[l], b_in[l], rel_bias, attn_sinks[l], cmp_pos_k[l], cmp_w1_k[l],
                           cmp_w2_k[l], cmp_pos_v[l], cmp_w1_v[l], cmp_w2_v[l], w_branch_a[l],
                           w_branch_b[l], w_out[l])
        h = layer_norm(DEEPNORM_ALPHA * h + mix, ln1_g[l], ln1_b[l])
        ffn = moe_ffn(h.reshape(bsz * seq, d), w_router[l], b_router[l], w_gate_up[l],
                      b_gate_up[l], w_down[l], b_down[l]).reshape(bsz, seq, d)
        h = layer_norm(DEEPNORM_ALPHA * h + ffn, ln2_g[l], ln2_b[l])
    return h
```

```python
import functools
import math

import jax
import jax.numpy as jnp
import numpy as np
from jax import lax
from jax.experimental import pallas as pl
from jax.experimental.pallas import tpu as pltpu

F32 = jnp.float32
BF16 = jnp.bfloat16

D_MODEL = 1024
SEQ = 2048
HEAD_DIM = 64
N_HEADS = 8
N_KV = 2
A_WINDOW = 128
B_WINDOW = 512
CMP_LEN = 32
CMP_STRIDE = 16
CMP_HIDDEN = 128
SLC_LEN = 64
SLC_TOP = 8
SLC_LOCAL = 2
N_BUCKETS = 32
REL_MAX_DIST = 128
N_EXPERTS = 32
TOP_K = 4
D_FF = D_MODEL
SWIGLU_LIMIT = 7.0
SWIGLU_ALPHA = 1.702
LN_EPS = 1e-5
NEG_INF = -1e30
FORCED_SCORE = 1e30
DEEPNORM_ALPHA = 2.0 ** 0.25
SCALE = HEAD_DIM ** -0.5

LANES = 128
QB = 128
N_QB = SEQ // QB
N_CHUNK = SEQ // CMP_STRIDE
N_SLC = SEQ // SLC_LEN
VMEM_LIMIT = 56 * 1024 * 1024

IN_OFF = dict(qa=0, ka=512, va=640, qb=768, kbc=1280, vbc=1408, kbs=1536, vbs=1664,
              kbw=1792, vbw=1920, gate=2048, merge=2072)
PROJ_TM = 512
MOE_TB = 2048
MOE_ROWS = 128


def _cparams(sem):
    return pltpu.CompilerParams(dimension_semantics=sem, vmem_limit_bytes=VMEM_LIMIT)


def _dot(a, b):
    return jnp.dot(a, b, preferred_element_type=F32)


def _dot_nt(a, b):
    return lax.dot_general(a, b, (((1,), (1,)), ((), ())), preferred_element_type=F32)


def _split_bf16(x):
    hi = x.astype(BF16)
    lo = (x - hi.astype(F32)).astype(BF16)
    return hi, lo


def _proj_kernel(x_ref, w_ref, b_ref, q_ref, k_ref, v_ref, g_ref, m_ref, c_ref):
    xb = x_ref[...].astype(BF16)

    def mm(c0, c1):
        return _dot(xb, w_ref[:, c0:c1]) + b_ref[:, c0:c1]

    for c in range(0, 1024, 512):
        q_ref[:, c:c + 512] = mm(c, c + 512).astype(BF16)
    k_ref[...] = mm(1024, 1792).astype(BF16)
    v_ref[...] = mm(1792, 2560).astype(BF16)
    g_ref[...] = jax.nn.sigmoid(mm(2560, 2688))
    for c in range(0, 2048, 512):
        m_ref[:, c:c + 512] = jax.nn.sigmoid(mm(2688 + c, 2688 + c + 512)).astype(BF16)
    c_ref[...] = mm(4736, 4992)


def _pack_in_weights(w_in, b_in):
    def cols(name, width):
        o = IN_OFF[name]
        return w_in[:, o:o + width], b_in[o:o + width]

    def dup_groups(name):
        w, b = cols(name, 128)
        ws, bs = [], []
        for g in range(N_KV):
            wg, bg = w[:, g * 64:(g + 1) * 64], b[g * 64:(g + 1) * 64]
            ws += [wg, wg]
            bs += [bg, bg]
        return jnp.concatenate(ws, axis=1), jnp.concatenate(bs)

    parts = [cols('qa', 512), cols('qb', 512),
             dup_groups('ka'), dup_groups('kbs'), dup_groups('kbw'),
             dup_groups('va'), dup_groups('vbs'), dup_groups('vbw')]
    wg, bg = cols('gate', 24)
    parts.append((jnp.pad(wg, ((0, 0), (0, 104))), jnp.pad(bg, (0, 104))))
    parts.append(cols('merge', 2048))
    parts.append(cols('kbc', 128))
    parts.append(cols('vbc', 128))
    w = jnp.concatenate([p[0] for p in parts], axis=1).astype(BF16)
    b = jnp.concatenate([p[1] for p in parts])[None, :]
    return w, b


def _in_projection(x2d, w_packed, b_packed):
    n_tok = x2d.shape[0]
    n_col = w_packed.shape[1]
    tm = PROJ_TM
    row = lambda i: (i, 0)
    fixed = lambda i: (0, 0)
    widths = (1024, 768, 768, 128, 2048, 256)
    dtypes = (BF16, BF16, BF16, F32, BF16, F32)
    return pl.pallas_call(
        _proj_kernel,
        grid=(n_tok // tm,),
        in_specs=[pl.BlockSpec((tm, D_MODEL), row),
                  pl.BlockSpec((D_MODEL, n_col), fixed),
                  pl.BlockSpec((1, n_col), fixed)],
        out_specs=[pl.BlockSpec((tm, w), row) for w in widths],
        out_shape=[jax.ShapeDtypeStruct((n_tok, w), dt) for w, dt in zip(widths, dtypes)],
        compiler_params=_cparams(("arbitrary",)),
    )(x2d, w_packed, b_packed)


def _bucket_np(rel):
    n = np.maximum(rel, 0)
    max_exact = N_BUCKETS // 2
    nf = np.maximum(n, 1).astype(np.float32)
    large = max_exact + (np.log(nf / max_exact) / math.log(REL_MAX_DIST / max_exact)
                         * (N_BUCKETS - max_exact)).astype(np.int32)
    large = np.minimum(large, N_BUCKETS - 1)
    return np.where(n < max_exact, n, large)


def _band_bias(bias_heads, window, n_prev):
    w = (n_prev + 1) * QB
    v = np.arange(n_prev + 1)[:, None, None]
    rel = v * QB + np.arange(QB)[None, :, None] - np.arange(w)[None, None, :]
    valid = (rel >= 0) & (rel < window)
    tab = bias_heads[_bucket_np(rel)]
    tab = jnp.transpose(tab, (0, 3, 1, 2))
    return jnp.where(valid[:, None], tab, NEG_INF).astype(F32)


def _slc_bias(bias_heads):
    d = np.arange(3)[:, None, None]
    rel = d * QB + np.arange(QB)[None, :, None] - np.arange(QB)[None, None, :]
    tab = jnp.transpose(bias_heads[_bucket_np(rel)], (3, 0, 1, 2))
    return jnp.where((rel >= 0)[None], tab, NEG_INF).astype(F32)


def _banded_kernel(*refs, n_prev, has_sinks):
    if has_sinks:
        q_ref, k_ref, v_ref, bias_ref, sink_ref, o_ref = refs
    else:
        q_ref, k_ref, v_ref, bias_ref, o_ref = refs
    i = pl.program_id(0) % N_QB
    w = (n_prev + 1) * QB
    start = pl.multiple_of(jnp.maximum(i - n_prev, 0) * QB, QB)
    lo = lax.broadcasted_iota(jnp.int32, (1, LANES), 1) < HEAD_DIM
    for g in range(N_KV):
        kc = k_ref[pl.ds(start, w), g * LANES:(g + 1) * LANES]
        vc = v_ref[pl.ds(start, w), g * LANES:(g + 1) * LANES]
        v_half = (jnp.where(lo, vc, 0), jnp.where(lo, 0, vc))
        for c in range(2):
            col = (2 * g + c) * LANES
            q2 = q_ref[:, col:col + LANES] * SCALE
            acc = jnp.zeros((QB, LANES), F32)
            for hh in range(2):
                h = 4 * g + 2 * c + hh
                qm = jnp.where(lo, q2, 0) if hh == 0 else jnp.where(lo, 0, q2)
                s = _dot_nt(qm, kc) + bias_ref[0, h]
                m = jnp.max(s, axis=1, keepdims=True)
                if has_sinks:
                    sk = sink_ref[h:h + 1, 0:1]
                    m = jnp.maximum(m, sk)
                e = jnp.exp(s - m)
                den = jnp.sum(e, axis=1, keepdims=True)
                if has_sinks:
                    den = den + jnp.exp(sk - m)
                p = (e / den).astype(BF16)
                acc = acc + _dot(p, v_half[hh])
            o_ref[:, col:col + LANES] = acc.astype(BF16)


def _banded_attention(q_all, k_all, v_all, bias, sinks, q_col, kv_col, n_prev):
    n_tok = q_all.shape[0]
    has_sinks = sinks is not None
    w = (n_prev + 1) * QB
    in_specs = [pl.BlockSpec((QB, 512), lambda gi: (gi, q_col)),
                pl.BlockSpec((SEQ, 256), lambda gi: (gi // N_QB, kv_col)),
                pl.BlockSpec((SEQ, 256), lambda gi: (gi // N_QB, kv_col)),
                pl.BlockSpec((1, N_HEADS, QB, w),
                             lambda gi: (jnp.minimum(gi % N_QB, n_prev), 0, 0, 0))]
    args = [q_all, k_all, v_all, bias]
    if has_sinks:
        in_specs.append(pl.BlockSpec((N_HEADS, LANES), lambda gi: (0, 0)))
        args.append(sinks)
    return pl.pallas_call(
        functools.partial(_banded_kernel, n_prev=n_prev, has_sinks=has_sinks),
        grid=(n_tok // QB,),
        in_specs=in_specs,
        out_specs=pl.BlockSpec((QB, 512), lambda gi: (gi, 0)),
        out_shape=jax.ShapeDtypeStruct((n_tok, 512), BF16),
        compiler_params=_cparams(("arbitrary",)),
    )(*args)


def _compress_kernel(z_ref, pos_ref, w1_ref, w2_ref, o_ref):
    half = CMP_STRIDE * HEAD_DIM
    for kv in range(2):
        for g in range(N_KV):
            z = z_ref[0, 2 * kv + g]
            za_h, za_l = _split_bf16(z + pos_ref[kv, :, 0:half])
            zb_h, zb_l = _split_bf16(z + pos_ref[kv, :, half:2 * half])
            w1a_h, w1a_l = w1_ref[kv, 0, 0:half], w1_ref[kv, 1, 0:half]
            w1b_h, w1b_l = w1_ref[kv, 0, half:2 * half], w1_ref[kv, 1, half:2 * half]
            ha = _dot(za_h, w1a_h) + _dot(za_l, w1a_h) + _dot(za_h, w1a_l)
            hb = _dot(zb_h, w1b_h) + _dot(zb_l, w1b_h) + _dot(zb_h, w1b_l)
            h = ha + pltpu.roll(hb, N_CHUNK - 1, axis=0)
            a = jax.nn.gelu(h)
            a_h, a_l = _split_bf16(a)
            out = (_dot(a_h, w2_ref[kv, 0]) + _dot(a_l, w2_ref[kv, 0]) + _dot(a_h, w2_ref[kv, 1]))
            o_ref[0, 2 * kv + g] = out.astype(BF16)


def _compress(c_out, bsz, cmp_pos_k, cmp_w1_k, cmp_w2_k, cmp_pos_v, cmp_w1_v, cmp_w2_v):
    z = c_out.reshape(bsz, N_CHUNK, CMP_STRIDE, 4, HEAD_DIM)
    z = jnp.transpose(z, (0, 3, 1, 2, 4)).reshape(bsz, 4, N_CHUNK, CMP_STRIDE * HEAD_DIM)
    pos = jnp.stack([cmp_pos_k.reshape(1, -1), cmp_pos_v.reshape(1, -1)])

    def split(w):
        hi = w.astype(BF16)
        return jnp.stack([hi, (w - hi.astype(F32)).astype(BF16)])

    w1 = jnp.stack([split(cmp_w1_k), split(cmp_w1_v)])
    w2 = jnp.stack([split(jnp.concatenate([cmp_w2_k, cmp_w2_k], axis=1)),
                    split(jnp.concatenate([cmp_w2_v, cmp_w2_v], axis=1))])
    return pl.pallas_call(
        _compress_kernel,
        grid=(bsz,),
        in_specs=[pl.BlockSpec((1, 4, N_CHUNK, 1024), lambda b: (b, 0, 0, 0)),
                  pl.BlockSpec((2, 1, 2048), lambda b: (0, 0, 0)),
                  pl.BlockSpec((2, 2, 2048, CMP_HIDDEN), lambda b: (0, 0, 0, 0)),
                  pl.BlockSpec((2, 2, CMP_HIDDEN, LANES), lambda b: (0, 0, 0, 0))],
        out_specs=pl.BlockSpec((1, 4, N_CHUNK, LANES), lambda b: (b, 0, 0, 0)),
        out_shape=jax.ShapeDtypeStruct((bsz, 4, N_CHUNK, LANES), BF16),
        compiler_params=_cparams(("arbitrary",)),
    )(z, pos, w1, w2)


CMP_QB = 256


def _cmp_select_kernel(q_ref, kv_ref, ov_ref, o_ref, sel_ref):
    i = pl.program_id(0) % (SEQ // CMP_QB)
    lane = lax.broadcasted_iota(jnp.int32, (1, LANES), 1)
    lo = lane < HEAD_DIM
    t = i * CMP_QB + lax.broadcasted_iota(jnp.int32, (CMP_QB, 1), 0)
    valid = (lane * CMP_STRIDE + (CMP_LEN - 1)) <= t
    any_valid = t >= CMP_LEN - 1
    cur = lax.shift_right_logical(t, int(math.log2(SLC_LEN)))
    forced = (lane == 0) | ((lane <= cur) & (lane > cur - SLC_LOCAL))
    future = lane > cur
    lane_f = lane.astype(F32)
    for g in range(N_KV):
        kc = kv_ref[0, g]
        vc = kv_ref[0, 2 + g]
        v_half = (jnp.where(lo, vc, 0), jnp.where(lo, 0, vc))
        psum = jnp.zeros((CMP_QB, LANES), F32)
        for c in range(2):
            col = (2 * g + c) * LANES
            q2 = q_ref[:, col:col + LANES] * SCALE
            acc = jnp.zeros((CMP_QB, LANES), F32)
            for hh in range(2):
                qm = jnp.where(lo, q2, 0) if hh == 0 else jnp.where(lo, 0, q2)
                s = jnp.where(valid, _dot_nt(qm, kc), NEG_INF)
                m = jnp.max(s, axis=1, keepdims=True)
                e = jnp.exp(s - m)
                p = e / jnp.sum(e, axis=1, keepdims=True)
                p = jnp.where(any_valid, p, 0.0)
                psum = psum + p
                acc = acc + _dot(p.astype(BF16), v_half[hh])
            o_ref[:, col:col + LANES] = acc.astype(BF16)
        p_h, p_l = _split_bf16(psum)
        imp = _dot(p_h, ov_ref[...]) + _dot(p_l, ov_ref[...])
        imp = jnp.where(forced, FORCED_SCORE, imp)
        imp = jnp.where(future, NEG_INF, imp)
        imp = jnp.where(lane < N_SLC, imp, -3e38)
        sel = jnp.zeros((CMP_QB, LANES), F32)
        for _ in range(SLC_TOP):
            mx = jnp.max(imp, axis=1, keepdims=True)
            idx = jnp.min(jnp.where(imp == mx, lane_f, 1e9), axis=1, keepdims=True)
            hit = lane_f == idx
            sel = jnp.where(hit, 1.0, sel)
            imp = jnp.where(hit, -3e38, imp)
        sel_ref[:, g * LANES:(g + 1) * LANES] = sel.astype(BF16)


def _cmp_overlap_matrix():
    nc = (SEQ - CMP_LEN) // CMP_STRIDE + 1
    cs = np.arange(nc)[:, None] * CMP_STRIDE
    ss = np.arange(N_SLC)[None, :] * SLC_LEN
    ov = np.clip(np.minimum(cs + CMP_LEN, ss + SLC_LEN) - np.maximum(cs, ss), 0, None)
    out = np.zeros((LANES, LANES), np.float32)
    out[:nc, :N_SLC] = ov / CMP_LEN
    return jnp.asarray(out, BF16)


def _cmp_select(q_all, kv_cmp):
    n_tok = q_all.shape[0]
    nqb = SEQ // CMP_QB
    return pl.pallas_call(
        _cmp_select_kernel,
        grid=(n_tok // CMP_QB,),
        in_specs=[pl.BlockSpec((CMP_QB, 512), lambda gi: (gi, 1)),
                  pl.BlockSpec((1, 4, N_CHUNK, LANES), lambda gi: (gi // nqb, 0, 0, 0)),
                  pl.BlockSpec((LANES, LANES), lambda gi: (0, 0))],
        out_specs=[pl.BlockSpec((CMP_QB, 512), lambda gi: (gi, 0)),
                   pl.BlockSpec((CMP_QB, 256), lambda gi: (gi, 0))],
        out_shape=[jax.ShapeDtypeStruct((n_tok, 512), BF16),
                   jax.ShapeDtypeStruct((n_tok, 256), BF16)],
        compiler_params=_cparams(("arbitrary",)),
    )(q_all, kv_cmp, _cmp_overlap_matrix())


def _selected_kernel(q_ref, k_ref, v_ref, sel_ref, exp_ref, bias_ref, o_ref):
    i = pl.program_id(0) % N_QB
    lo = lax.broadcasted_iota(jnp.int32, (1, LANES), 1) < HEAD_DIM
    for g in range(N_KV):
        selg = sel_ref[:, g * LANES:(g + 1) * LANES]
        qms = []
        for c in range(2):
            col = (2 * g + c) * LANES
            q2 = q_ref[:, col:col + LANES] * SCALE
            qms += [jnp.where(lo, q2, 0), jnp.where(lo, 0, q2)]

        def tile(j, carry):
            ms, ls, accs = carry
            ks = pl.multiple_of(j * QB, QB)
            kt = k_ref[pl.ds(ks, QB), g * LANES:(g + 1) * LANES]
            vt = v_ref[pl.ds(ks, QB), g * LANES:(g + 1) * LANES]
            v_half = (jnp.where(lo, vt, 0), jnp.where(lo, 0, vt))
            member = _dot(selg, exp_ref[j])
            mask_add = (member - 1.0) * (-NEG_INF)
            d = jnp.minimum(i - j, 2)
            new_ms, new_ls, new_accs = [], [], []
            for c in range(2):
                acc = accs[c]
                alphas = []
                pv = []
                for hh in range(2):
                    r = 2 * c + hh
                    h = 4 * g + r
                    s = _dot_nt(qms[r], kt) + bias_ref[h, pl.ds(d, 1)][0] + mask_add
                    m_new = jnp.maximum(ms[r], jnp.max(s, axis=1, keepdims=True))
                    alpha = jnp.exp(ms[r] - m_new)
                    e = jnp.exp(s - m_new)
                    new_ms.append(m_new)
                    new_ls.append(alpha * ls[r] + jnp.sum(e, axis=1, keepdims=True))
                    alphas.append(alpha)
                    pv.append(_dot(e.astype(BF16), v_half[hh]))
                acc = acc * jnp.where(lo, alphas[0], alphas[1]) + pv[0] + pv[1]
                new_accs.append(acc)
            return tuple(new_ms), tuple(new_ls), tuple(new_accs)

        init = (tuple(jnp.full((QB, 1), -3e38, F32) for _ in range(4)),
                tuple(jnp.zeros((QB, 1), F32) for _ in range(4)),
                tuple(jnp.zeros((QB, LANES), F32) for _ in range(2)))
        ms, ls, accs = lax.fori_loop(0, i + 1, tile, init)
        for c in range(2):
            col = (2 * g + c) * LANES
            inv = jnp.where(lo, 1.0 / ls[2 * c], 1.0 / ls[2 * c + 1])
            o_ref[:, col:col + LANES] = (accs[c] * inv).astype(BF16)


def _slc_expand_matrix():
    key_blk = (np.arange(N_QB)[:, None, None] * QB + np.arange(QB)[None, None, :]) // SLC_LEN
    e = (np.arange(LANES)[None, :, None] == key_blk).astype(np.float32)
    return jnp.asarray(e, BF16)


def _selected_attention(q_all, k_all, v_all, sel, bias):
    n_tok = q_all.shape[0]
    return pl.pallas_call(
        _selected_kernel,
        grid=(n_tok // QB,),
        in_specs=[pl.BlockSpec((QB, 512), lambda gi: (gi, 1)),
                  pl.BlockSpec((SEQ, 256), lambda gi: (gi // N_QB, 1)),
                  pl.BlockSpec((SEQ, 256), lambda gi: (gi // N_QB, 1)),
                  pl.BlockSpec((QB, 256), lambda gi: (gi, 0)),
                  pl.BlockSpec((N_QB, LANES, QB), lambda gi: (0, 0, 0)),
                  pl.BlockSpec((N_HEADS, 3, QB, QB), lambda gi: (0, 0, 0, 0))],
        out_specs=pl.BlockSpec((QB, 512), lambda gi: (gi, 0)),
        out_shape=jax.ShapeDtypeStruct((n_tok, 512), BF16),
        compiler_params=_cparams(("arbitrary",)),
    )(q_all, k_all, v_all, sel, _slc_expand_matrix(), bias)


MIX_TM = 512


def _layer_norm(x, g, b):
    mu = jnp.mean(x, axis=-1, keepdims=True)
    xc = x - mu
    var = jnp.mean(xc * xc, axis=-1, keepdims=True)
    return xc * lax.rsqrt(var + LN_EPS) * g + b


def _mix_kernel(x_ref, ya_ref, oc_ref, os_ref, ow_ref, g_ref, m_ref, ge_ref, wa_ref, wb_ref,
                wo_ref, lng_ref, lnb_ref, wr_ref, br_ref, h_ref, hb_ref, gt_ref):
    gb = g_ref[...].astype(BF16)
    yb = (_dot(gb, ge_ref[0]) * oc_ref[...].astype(F32)
          + _dot(gb, ge_ref[1]) * os_ref[...].astype(F32)
          + _dot(gb, ge_ref[2]) * ow_ref[...].astype(F32))
    ma = _dot(ya_ref[...], wa_ref[...])
    mb = _dot(yb.astype(BF16), wb_ref[...])
    merged = m_ref[:, 0:D_MODEL].astype(F32) * ma + m_ref[:, D_MODEL:2 * D_MODEL].astype(F32) * mb
    mix = _dot(merged.astype(BF16), wo_ref[...])
    h = _layer_norm(DEEPNORM_ALPHA * x_ref[...] + mix, lng_ref[...], lnb_ref[...])
    h_ref[...] = h
    hb_ref[...] = h.astype(BF16)
    h_hi, h_lo = _split_bf16(h)
    logits = (_dot_nt(wr_ref[0], h_hi) + _dot_nt(wr_ref[0], h_lo) + _dot_nt(wr_ref[1], h_hi)
              + br_ref[...])
    row = lax.broadcasted_iota(jnp.int32, logits.shape, 0).astype(F32)
    vals, hits = [], []
    v = logits
    for _ in range(TOP_K):
        mx = jnp.max(v, axis=0, keepdims=True)
        idx = jnp.min(jnp.where(v == mx, row, 1e9), axis=0, keepdims=True)
        hit = row == idx
        vals.append(mx)
        hits.append(hit)
        v = jnp.where(hit, -3e38, v)
    es = [jnp.exp(t - vals[0]) for t in vals]
    den = es[0] + es[1] + es[2] + es[3]
    gt = jnp.zeros(logits.shape, F32)
    for k in range(TOP_K):
        gt = jnp.where(hits[k], es[k] / den, gt)
    gt_ref[...] = gt


def _gate_expand_matrix():
    e = np.zeros((3, LANES, 512), np.float32)
    for c in range(3):
        for h in range(N_HEADS):
            e[c, 3 * h + c, h * HEAD_DIM:(h + 1) * HEAD_DIM] = 1.0
    return jnp.asarray(e, BF16)


def _mix(x2d, ya, oc, osl, ow, gates, merge, w_branch_a, w_branch_b, w_out, ln_g, ln_b,
         w_router, b_router):
    n_tok = x2d.shape[0]
    tm = MIX_TM
    row = lambda i: (i, 0)
    fix2 = lambda i: (0, 0)
    fix3 = lambda i: (0, 0, 0)
    wr_t = w_router.T
    wr_hi = wr_t.astype(BF16)
    wr = jnp.stack([wr_hi, (wr_t - wr_hi.astype(F32)).astype(BF16)])
    return pl.pallas_call(
        _mix_kernel,
        grid=(n_tok // tm,),
        in_specs=[pl.BlockSpec((tm, D_MODEL), row),
                  pl.BlockSpec((tm, 512), row), pl.BlockSpec((tm, 512), row),
                  pl.BlockSpec((tm, 512), row), pl.BlockSpec((tm, 512), row),
                  pl.BlockSpec((tm, LANES), row), pl.BlockSpec((tm, 2 * D_MODEL), row),
                  pl.BlockSpec((3, LANES, 512), fix3),
                  pl.BlockSpec((512, D_MODEL), fix2), pl.BlockSpec((512, D_MODEL), fix2),
                  pl.BlockSpec((D_MODEL, D_MODEL), fix2),
                  pl.BlockSpec((1, D_MODEL), fix2), pl.BlockSpec((1, D_MODEL), fix2),
                  pl.BlockSpec((2, N_EXPERTS, D_MODEL), fix3),
                  pl.BlockSpec((N_EXPERTS, 1), fix2)],
        out_specs=[pl.BlockSpec((tm, D_MODEL), row), pl.BlockSpec((tm, D_MODEL), row),
                   pl.BlockSpec((N_EXPERTS, tm), lambda i: (0, i))],
        out_shape=[jax.ShapeDtypeStruct((n_tok, D_MODEL), F32),
                   jax.ShapeDtypeStruct((n_tok, D_MODEL), BF16),
                   jax.ShapeDtypeStruct((N_EXPERTS, n_tok), F32)],
        compiler_params=_cparams(("arbitrary",)),
    )(x2d, ya, oc, osl, ow, gates, merge, _gate_expand_matrix(),
      w_branch_a.astype(BF16), w_branch_b.astype(BF16), w_out.astype(BF16),
      ln_g[None, :], ln_b[None, :], wr, b_router[:, None])


MOE_SUB = 256


def _moe_kernel(gt_ref, x_ref, tri_ref, wgu_ref, bgu_ref, wd_ref, bd_ref, o_ref, pos_ref):
    e = pl.program_id(1)

    @pl.when(e == 0)
    def _():
        o_ref[...] = jnp.zeros(o_ref.shape, F32)
        carry = jnp.zeros((N_EXPERTS, 1), F32)
        for sb in range(MOE_TB // MOE_SUB):
            sl = slice(sb * MOE_SUB, (sb + 1) * MOE_SUB)
            hit = gt_ref[:, sl] > 0.0
            msk = jnp.where(hit, 1.0, 0.0)
            pos = _dot(msk.astype(BF16), tri_ref[...]) + carry
            pos_ref[:, sl] = jnp.where(hit, pos, -1.0)
            carry = carry + jnp.sum(msk, axis=1, keepdims=True)

    grow = gt_ref[pl.ds(e, 1), :]
    prow = pos_ref[pl.ds(e, 1), :]
    cnt = jnp.sum((grow > 0.0).astype(jnp.int32))
    n_chunks = (cnt + (MOE_ROWS - 1)) // MOE_ROWS
    rid = lax.broadcasted_iota(jnp.int32, (MOE_ROWS, 1), 0).astype(F32)

    def chunk(c, carry):
        onehot = prow == (rid + (c * MOE_ROWS).astype(F32))
        xc = _dot(jnp.where(onehot, 1.0, 0.0).astype(BF16), x_ref[...]).astype(BF16)
        hcat = _dot(xc, wgu_ref[0]) + bgu_ref[0]
        glu = jnp.minimum(hcat[:, 0:D_FF], SWIGLU_LIMIT)
        lin = jnp.clip(hcat[:, D_FF:2 * D_FF], -SWIGLU_LIMIT, SWIGLU_LIMIT)
        act = glu * jax.nn.sigmoid(SWIGLU_ALPHA * glu) * (lin + 1.0)
        y = _dot(act.astype(BF16), wd_ref[0]) + bd_ref[0]
        pw = jnp.where(onehot, grow, 0.0).astype(BF16)
        o_ref[...] += lax.dot_general(pw, y.astype(BF16), (((0,), (0,)), ((), ())),
                                      preferred_element_type=F32)
        return carry

    lax.fori_loop(0, n_chunks, chunk, 0)


def _moe(gt, hb, w_gate_up, b_gate_up, w_down, b_down):
    n_tok = hb.shape[0]
    tri = jnp.asarray(np.triu(np.ones((MOE_SUB, MOE_SUB), np.float32), k=1), BF16)
    return pl.pallas_call(
        _moe_kernel,
        grid=(n_tok // MOE_TB, N_EXPERTS),
        in_specs=[pl.BlockSpec((N_EXPERTS, MOE_TB), lambda t, e: (0, t)),
                  pl.BlockSpec((MOE_TB, D_MODEL), lambda t, e: (t, 0)),
                  pl.BlockSpec((MOE_SUB, MOE_SUB), lambda t, e: (0, 0)),
                  pl.BlockSpec((1, D_MODEL, 2 * D_FF), lambda t, e: (e, 0, 0)),
                  pl.BlockSpec((1, 1, 2 * D_FF), lambda t, e: (e, 0, 0)),
                  pl.BlockSpec((1, D_FF, D_MODEL), lambda t, e: (e, 0, 0)),
                  pl.BlockSpec((1, 1, D_MODEL), lambda t, e: (e, 0, 0))],
        out_specs=pl.BlockSpec((MOE_TB, D_MODEL), lambda t, e: (t, 0)),
        out_shape=jax.ShapeDtypeStruct((n_tok, D_MODEL), F32),
        scratch_shapes=[pltpu.VMEM((N_EXPERTS, MOE_TB), F32)],
        compiler_params=_cparams(("arbitrary", "arbitrary")),
    )(gt, hb, tri, w_gate_up.astype(BF16), b_gate_up[:, None, :], w_down.astype(BF16),
      b_down[:, None, :])


def _ln_kernel(h_ref, f_ref, g_ref, b_ref, o_ref):
    o_ref[...] = _layer_norm(DEEPNORM_ALPHA * h_ref[...] + f_ref[...], g_ref[...], b_ref[...])


def _final_ln(h, ffn, g, b):
    n_tok = h.shape[0]
    tm = 1024
    row = lambda i: (i, 0)
    fix = lambda i: (0, 0)
    return pl.pallas_call(
        _ln_kernel,
        grid=(n_tok // tm,),
        in_specs=[pl.BlockSpec((tm, D_MODEL), row), pl.BlockSpec((tm, D_MODEL), row),
                  pl.BlockSpec((1, D_MODEL), fix), pl.BlockSpec((1, D_MODEL), fix)],
        out_specs=pl.BlockSpec((tm, D_MODEL), row),
        out_shape=jax.ShapeDtypeStruct((n_tok, D_MODEL), F32),
        compiler_params=_cparams(("arbitrary",)),
    )(h, ffn, g[None, :], b[None, :])


def _layer(x2d, bsz, w_in, b_in, rel_bias, attn_sinks, cmp_pos_k, cmp_w1_k, cmp_w2_k, cmp_pos_v,
           cmp_w1_v, cmp_w2_v, w_branch_a, w_branch_b, w_out, ln1_g, ln1_b, w_router, b_router,
           w_gate_up, b_gate_up, w_down, b_down, ln2_g, ln2_b):
    w_packed, b_packed = _pack_in_weights(w_in, b_in)
    q_all, k_all, v_all, gates, merge, c_out = _in_projection(x2d, w_packed, b_packed)

    bias_a = rel_bias[:, :N_HEADS]
    bias_b = rel_bias[:, N_HEADS:]
    sinks = jnp.broadcast_to(attn_sinks[:, None], (N_HEADS, LANES))
    ya = _banded_attention(q_all, k_all, v_all, _band_bias(bias_a, A_WINDOW, 1), sinks,
                           q_col=0, kv_col=0, n_prev=1)
    ow = _banded_attention(q_all, k_all, v_all, _band_bias(bias_b, B_WINDOW, 4), None,
                           q_col=1, kv_col=2, n_prev=4)
    kv_cmp = _compress(c_out, bsz, cmp_pos_k, cmp_w1_k, cmp_w2_k, cmp_pos_v, cmp_w1_v, cmp_w2_v)
    oc, sel = _cmp_select(q_all, kv_cmp)
    osl = _selected_attention(q_all, k_all, v_all, sel, _slc_bias(bias_b))

    h, hb, gt = _mix(x2d, ya, oc, osl, ow, gates, merge, w_branch_a, w_branch_b, w_out,
                     ln1_g, ln1_b, w_router, b_router)
    ffn = _moe(gt, hb, w_gate_up, b_gate_up, w_down, b_down)
    return _final_ln(h, ffn, ln2_g, ln2_b)


def kernel(x, w_in, b_in, rel_bias, attn_sinks, cmp_pos_k, cmp_w1_k, cmp_w2_k, cmp_pos_v, cmp_w1_v,
           cmp_w2_v, w_branch_a, w_branch_b, w_out, ln1_g, ln1_b, w_router, b_router, w_gate_up,
           b_gate_up, w_down, b_down, ln2_g, ln2_b):
    bsz, seq, d = x.shape
    assert seq == SEQ and d == D_MODEL
    h = x.reshape(bsz * seq, d)
    for l in range(w_in.shape[0]):
        h = _layer(h, bsz, w_in[l], b_in[l], rel_bias, attn_sinks[l], cmp_pos_k[l], cmp_w1_k[l],
                   cmp_w2_k[l], cmp_pos_v[l], cmp_w1_v[l], cmp_w2_v[l], w_branch_a[l],
                   w_branch_b[l], w_out[l], ln1_g[l], ln1_b[l], w_router[l], b_router[l],
                   w_gate_up[l], b_gate_up[l], w_down[l], b_down[l], ln2_g[l], ln2_b[l])
    return h.reshape(bsz, seq, d)
```

```python
import functools
import math

import jax
import jax.numpy as jnp
import numpy as np
from jax import lax
from jax.experimental import pallas as pl
from jax.experimental.pallas import tpu as pltpu

F32 = jnp.float32
BF16 = jnp.bfloat16

D_MODEL = 1024
SEQ = 2048
HEAD_DIM = 64
N_HEADS = 8
N_KV = 2
A_WINDOW = 128
B_WINDOW = 512
CMP_LEN = 32
CMP_STRIDE = 16
CMP_HIDDEN = 128
SLC_LEN = 64
SLC_TOP = 8
SLC_LOCAL = 2
N_BUCKETS = 32
REL_MAX_DIST = 128
N_EXPERTS = 32
TOP_K = 4
D_FF = D_MODEL
SWIGLU_LIMIT = 7.0
SWIGLU_ALPHA = 1.702
LN_EPS = 1e-5
NEG_INF = -1e30
FORCED_SCORE = 1e30
DEEPNORM_ALPHA = 2.0 ** 0.25
SCALE = HEAD_DIM ** -0.5

LANES = 128
QB = 128
N_QB = SEQ // QB
SLC_QB = 256
N_SQB = SEQ // SLC_QB
N_CHUNK = SEQ // CMP_STRIDE
N_SLC = SEQ // SLC_LEN
VMEM_LIMIT = 56 * 1024 * 1024

IN_OFF = dict(qa=0, ka=512, va=640, qb=768, kbc=1280, vbc=1408, kbs=1536, vbs=1664,
              kbw=1792, vbw=1920, gate=2048, merge=2072)
PROJ_TM = 512
MOE_TB = 2048
MOE_ROWS = 128


def _cparams(sem):
    return pltpu.CompilerParams(dimension_semantics=sem, vmem_limit_bytes=VMEM_LIMIT)


def _dot(a, b):
    return jnp.dot(a, b, preferred_element_type=F32)


def _dot_nt(a, b):
    return lax.dot_general(a, b, (((1,), (1,)), ((), ())), preferred_element_type=F32)


def _split_bf16(x):
    hi = x.astype(BF16)
    lo = (x - hi.astype(F32)).astype(BF16)
    return hi, lo


def _proj_kernel(x_ref, w_ref, b_ref, q_ref, k_ref, v_ref, g_ref, m_ref, c_ref):
    xb = x_ref[...].astype(BF16)

    def mm(c0, c1):
        return _dot(xb, w_ref[:, c0:c1]) + b_ref[:, c0:c1]

    for c in range(0, 1024, 512):
        q_ref[:, c:c + 512] = mm(c, c + 512).astype(BF16)
    k_ref[...] = mm(1024, 1792).astype(BF16)
    v_ref[...] = mm(1792, 2560).astype(BF16)
    g_ref[...] = jax.nn.sigmoid(mm(2560, 2688))
    for c in range(0, 2048, 512):
        m_ref[:, c:c + 512] = jax.nn.sigmoid(mm(2688 + c, 2688 + c + 512)).astype(BF16)
    c_ref[...] = mm(4736, 4992)


def _pack_in_weights(w_in, b_in):
    def cols(name, width):
        o = IN_OFF[name]
        return w_in[:, o:o + width], b_in[o:o + width]

    def dup_groups(name):
        w, b = cols(name, 128)
        ws, bs = [], []
        for g in range(N_KV):
            wg, bg = w[:, g * 64:(g + 1) * 64], b[g * 64:(g + 1) * 64]
            ws += [wg, wg]
            bs += [bg, bg]
        return jnp.concatenate(ws, axis=1), jnp.concatenate(bs)

    parts = [cols('qa', 512), cols('qb', 512),
             dup_groups('ka'), dup_groups('kbs'), dup_groups('kbw'),
             dup_groups('va'), dup_groups('vbs'), dup_groups('vbw')]
    wg, bg = cols('gate', 24)
    parts.append((jnp.pad(wg, ((0, 0), (0, 104))), jnp.pad(bg, (0, 104))))
    parts.append(cols('merge', 2048))
    parts.append(cols('kbc', 128))
    parts.append(cols('vbc', 128))
    w = jnp.concatenate([p[0] for p in parts], axis=1).astype(BF16)
    b = jnp.concatenate([p[1] for p in parts])[None, :]
    return w, b


def _in_projection(x2d, w_packed, b_packed):
    n_tok = x2d.shape[0]
    n_col = w_packed.shape[1]
    tm = PROJ_TM
    row = lambda i: (i, 0)
    fixed = lambda i: (0, 0)
    widths = (1024, 768, 768, 128, 2048, 256)
    dtypes = (BF16, BF16, BF16, F32, BF16, F32)
    return pl.pallas_call(
        _proj_kernel,
        grid=(n_tok // tm,),
        in_specs=[pl.BlockSpec((tm, D_MODEL), row),
                  pl.BlockSpec((D_MODEL, n_col), fixed),
                  pl.BlockSpec((1, n_col), fixed)],
        out_specs=[pl.BlockSpec((tm, w), row) for w in widths],
        out_shape=[jax.ShapeDtypeStruct((n_tok, w), dt) for w, dt in zip(widths, dtypes)],
        compiler_params=_cparams(("arbitrary",)),
    )(x2d, w_packed, b_packed)


def _bucket_np(rel):
    n = np.maximum(rel, 0)
    max_exact = N_BUCKETS // 2
    nf = np.maximum(n, 1).astype(np.float32)
    large = max_exact + (np.log(nf / max_exact) / math.log(REL_MAX_DIST / max_exact)
                         * (N_BUCKETS - max_exact)).astype(np.int32)
    large = np.minimum(large, N_BUCKETS - 1)
    return np.where(n < max_exact, n, large)


def _bias_tiles(bias_heads, rel, valid):
    n_var, rows, cols = rel.shape
    onehot = (jnp.asarray(_bucket_np(rel), jnp.int32)[..., None]
              == jnp.arange(N_BUCKETS, dtype=jnp.int32)).astype(F32)
    tab = jnp.einsum('vack,kh->vhac', onehot, bias_heads, precision=lax.Precision.HIGHEST)
    tab = jnp.where(jnp.asarray(valid)[:, None], tab, NEG_INF)
    return tab.reshape(n_var, N_KV, 4 * rows, cols)


def _band_bias(bias_heads, window, n_prev):
    w = (n_prev + 1) * QB
    v = np.arange(n_prev + 1)[:, None, None]
    rel = v * QB + np.arange(QB)[None, :, None] - np.arange(w)[None, None, :]
    return _bias_tiles(bias_heads, rel, (rel >= 0) & (rel < window))


def _slc_bias(bias_heads):
    d = np.arange(3)[:, None, None]
    rel = d * SLC_QB + np.arange(SLC_QB)[None, :, None] - np.arange(SLC_QB)[None, None, :]
    return _bias_tiles(bias_heads, rel, rel >= 0)


def _stack_heads(q_ref, rows, g, lo):
    parts = []
    for c in range(2):
        col = (2 * g + c) * LANES
        q2 = q_ref[rows, col:col + LANES] * SCALE
        parts += [jnp.where(lo, q2, 0), jnp.where(lo, 0, q2)]
    return jnp.concatenate(parts, axis=0)


def _unstack_heads(o, o_ref, rows, g, lo, n):
    for c in range(2):
        col = (2 * g + c) * LANES
        pair = jnp.where(lo, o[2 * c * n:(2 * c + 1) * n], o[(2 * c + 1) * n:(2 * c + 2) * n])
        o_ref[rows, col:col + LANES] = pair.astype(BF16)


def _banded_kernel(*refs, n_prev, n_sub, has_sinks):
    if has_sinks:
        q_ref, k_ref, v_ref, bias_ref, sink_ref, o_ref = refs
    else:
        q_ref, k_ref, v_ref, bias_ref, o_ref = refs
    w = (n_prev + 1) * QB
    lo = lax.broadcasted_iota(jnp.int32, (1, LANES), 1) < HEAD_DIM
    for u in range(n_sub):
        i = (pl.program_id(0) * n_sub + u) % N_QB
        start = pl.multiple_of(jnp.maximum(i - n_prev, 0) * QB, QB)
        var = jnp.minimum(i, n_prev)
        rows = slice(u * QB, (u + 1) * QB)
        for g in range(N_KV):
            kc = k_ref[pl.ds(start, w), g * LANES:(g + 1) * LANES]
            vc = v_ref[pl.ds(start, w), g * LANES:(g + 1) * LANES]
            s = _dot_nt(_stack_heads(q_ref, rows, g, lo), kc) + bias_ref[var, g]
            m = jnp.max(s, axis=1, keepdims=True)
            if has_sinks:
                sk = sink_ref[g][:, 0:1]
                m = jnp.maximum(m, sk)
            e = jnp.exp(s - m)
            den = jnp.sum(e, axis=1, keepdims=True)
            if has_sinks:
                den = den + jnp.exp(sk - m)
            o = _dot(e.astype(BF16), vc) / den
            _unstack_heads(o, o_ref, rows, g, lo, QB)


def _banded_attention(q_all, k_all, v_all, bias, sinks, q_col, kv_col, n_prev, n_sub):
    n_tok = q_all.shape[0]
    has_sinks = sinks is not None
    w = (n_prev + 1) * QB
    tq = QB * n_sub
    in_specs = [pl.BlockSpec((tq, 512), lambda gi: (gi, q_col)),
                pl.BlockSpec((SEQ, 256), lambda gi: (gi // (SEQ // tq), kv_col)),
                pl.BlockSpec((SEQ, 256), lambda gi: (gi // (SEQ // tq), kv_col)),
                pl.BlockSpec((n_prev + 1, N_KV, 4 * QB, w), lambda gi: (0, 0, 0, 0))]
    args = [q_all, k_all, v_all, bias]
    if has_sinks:
        in_specs.append(pl.BlockSpec((N_KV, 4 * QB, LANES), lambda gi: (0, 0, 0)))
        args.append(sinks)
    return pl.pallas_call(
        functools.partial(_banded_kernel, n_prev=n_prev, n_sub=n_sub, has_sinks=has_sinks),
        grid=(n_tok // tq,),
        in_specs=in_specs,
        out_specs=pl.BlockSpec((tq, 512), lambda gi: (gi, 0)),
        out_shape=jax.ShapeDtypeStruct((n_tok, 512), BF16),
        compiler_params=_cparams(("arbitrary",)),
    )(*args)


def _compress_kernel(z_ref, pos_ref, w1_ref, w2_ref, o_ref):
    half = CMP_STRIDE * HEAD_DIM
    for kv in range(2):
        for g in range(N_KV):
            z = z_ref[0, 2 * kv + g]
            za_h, za_l = _split_bf16(z + pos_ref[kv, :, 0:half])
            zb_h, zb_l = _split_bf16(z + pos_ref[kv, :, half:2 * half])
            w1a_h, w1a_l = w1_ref[kv, 0, 0:half], w1_ref[kv, 1, 0:half]
            w1b_h, w1b_l = w1_ref[kv, 0, half:2 * half], w1_ref[kv, 1, half:2 * half]
            ha = _dot(za_h, w1a_h) + _dot(za_l, w1a_h) + _dot(za_h, w1a_l)
            hb = _dot(zb_h, w1b_h) + _dot(zb_l, w1b_h) + _dot(zb_h, w1b_l)
            h = ha + pltpu.roll(hb, N_CHUNK - 1, axis=0)
            a = jax.nn.gelu(h)
            a_h, a_l = _split_bf16(a)
            out = (_dot(a_h, w2_ref[kv, 0]) + _dot(a_l, w2_ref[kv, 0]) + _dot(a_h, w2_ref[kv, 1]))
            o_ref[0, 2 * kv + g] = out.astype(BF16)


def _compress(c_out, bsz, cmp_pos_k, cmp_w1_k, cmp_w2_k, cmp_pos_v, cmp_w1_v, cmp_w2_v):
    z = c_out.reshape(bsz, N_CHUNK, CMP_STRIDE, 4, HEAD_DIM)
    z = jnp.transpose(z, (0, 3, 1, 2, 4)).reshape(bsz, 4, N_CHUNK, CMP_STRIDE * HEAD_DIM)
    pos = jnp.stack([cmp_pos_k.reshape(1, -1), cmp_pos_v.reshape(1, -1)])

    def split(w):
        hi = w.astype(BF16)
        return jnp.stack([hi, (w - hi.astype(F32)).astype(BF16)])

    w1 = jnp.stack([split(cmp_w1_k), split(cmp_w1_v)])
    w2 = jnp.stack([split(jnp.concatenate([cmp_w2_k, cmp_w2_k], axis=1)),
                    split(jnp.concatenate([cmp_w2_v, cmp_w2_v], axis=1))])
    return pl.pallas_call(
        _compress_kernel,
        grid=(bsz,),
        in_specs=[pl.BlockSpec((1, 4, N_CHUNK, 1024), lambda b: (b, 0, 0, 0)),
                  pl.BlockSpec((2, 1, 2048), lambda b: (0, 0, 0)),
                  pl.BlockSpec((2, 2, 2048, CMP_HIDDEN), lambda b: (0, 0, 0, 0)),
                  pl.BlockSpec((2, 2, CMP_HIDDEN, LANES), lambda b: (0, 0, 0, 0))],
        out_specs=pl.BlockSpec((1, 4, N_CHUNK, LANES), lambda b: (b, 0, 0, 0)),
        out_shape=jax.ShapeDtypeStruct((bsz, 4, N_CHUNK, LANES), BF16),
        compiler_params=_cparams(("arbitrary",)),
    )(z, pos, w1, w2)


CMP_QB = 256


def _cmp_select_kernel(q_ref, kv_ref, ov_ref, o_ref, sel_ref):
    i = pl.program_id(0) % (SEQ // CMP_QB)
    lane = lax.broadcasted_iota(jnp.int32, (1, LANES), 1)
    lo = lane < HEAD_DIM
    t = i * CMP_QB + lax.broadcasted_iota(jnp.int32, (CMP_QB, 1), 0)
    valid = (lane * CMP_STRIDE + (CMP_LEN - 1)) <= t
    any_valid = t >= CMP_LEN - 1
    cur = lax.shift_right_logical(t, int(math.log2(SLC_LEN)))
    forced = (lane == 0) | ((lane <= cur) & (lane > cur - SLC_LOCAL))
    future = lane > cur
    lane_f = lane.astype(F32)
    for g in range(N_KV):
        kc = kv_ref[0, g]
        vc = kv_ref[0, 2 + g]
        v_half = (jnp.where(lo, vc, 0), jnp.where(lo, 0, vc))
        psum = jnp.zeros((CMP_QB, LANES), F32)
        for c in range(2):
            col = (2 * g + c) * LANES
            q2 = q_ref[:, col:col + LANES] * SCALE
            acc = jnp.zeros((CMP_QB, LANES), F32)
            for hh in range(2):
                qm = jnp.where(lo, q2, 0) if hh == 0 else jnp.where(lo, 0, q2)
                s = jnp.where(valid, _dot_nt(qm, kc), NEG_INF)
                m = jnp.max(s, axis=1, keepdims=True)
                e = jnp.exp(s - m)
                p = e / jnp.sum(e, axis=1, keepdims=True)
                p = jnp.where(any_valid, p, 0.0)
                psum = psum + p
                acc = acc + _dot(p.astype(BF16), v_half[hh])
            o_ref[:, col:col + LANES] = acc.astype(BF16)
        p_h, p_l = _split_bf16(psum)
        imp = _dot(p_h, ov_ref[...]) + _dot(p_l, ov_ref[...])
        imp = jnp.where(forced, FORCED_SCORE, imp)
        imp = jnp.where(future, NEG_INF, imp)
        imp = jnp.where(lane < N_SLC, imp, -3e38)
        sel = jnp.zeros((CMP_QB, LANES), F32)
        for _ in range(SLC_TOP):
            mx = jnp.max(imp, axis=1, keepdims=True)
            idx = jnp.min(jnp.where(imp == mx, lane_f, 1e9), axis=1, keepdims=True)
            hit = lane_f == idx
            sel = jnp.where(hit, 1.0, sel)
            imp = jnp.where(hit, -3e38, imp)
        sel_ref[:, g * LANES:(g + 1) * LANES] = sel.astype(BF16)


def _cmp_overlap_matrix():
    nc = (SEQ - CMP_LEN) // CMP_STRIDE + 1
    cs = np.arange(nc)[:, None] * CMP_STRIDE
    ss = np.arange(N_SLC)[None, :] * SLC_LEN
    ov = np.clip(np.minimum(cs + CMP_LEN, ss + SLC_LEN) - np.maximum(cs, ss), 0, None)
    out = np.zeros((LANES, LANES), np.float32)
    out[:nc, :N_SLC] = ov / CMP_LEN
    return jnp.asarray(out, BF16)


def _cmp_select(q_all, kv_cmp):
    n_tok = q_all.shape[0]
    nqb = SEQ // CMP_QB
    return pl.pallas_call(
        _cmp_select_kernel,
        grid=(n_tok // CMP_QB,),
        in_specs=[pl.BlockSpec((CMP_QB, 512), lambda gi: (gi, 1)),
                  pl.BlockSpec((1, 4, N_CHUNK, LANES), lambda gi: (gi // nqb, 0, 0, 0)),
                  pl.BlockSpec((LANES, LANES), lambda gi: (0, 0))],
        out_specs=[pl.BlockSpec((CMP_QB, 512), lambda gi: (gi, 0)),
                   pl.BlockSpec((CMP_QB, 256), lambda gi: (gi, 0))],
        out_shape=[jax.ShapeDtypeStruct((n_tok, 512), BF16),
                   jax.ShapeDtypeStruct((n_tok, 256), BF16)],
        compiler_params=_cparams(("arbitrary",)),
    )(q_all, kv_cmp, _cmp_overlap_matrix())


def _selected_kernel(q_ref, k_ref, v_ref, sel_ref, exp_ref, bias_ref, o_ref):
    i = pl.program_id(0) % N_SQB
    lo = lax.broadcasted_iota(jnp.int32, (1, LANES), 1) < HEAD_DIM
    rows = slice(0, SLC_QB)
    qs = [_stack_heads(q_ref, rows, g, lo) for g in range(N_KV)]
    sels = [sel_ref[:, g * LANES:(g + 1) * LANES] for g in range(N_KV)]

    def tile(j, carry):
        ks = pl.multiple_of(j * SLC_QB, SLC_QB)
        d = jnp.minimum(i - j, 2)
        out = []
        for g in range(N_KV):
            m, l, acc = carry[g]
            kt = k_ref[pl.ds(ks, SLC_QB), g * LANES:(g + 1) * LANES]
            vt = v_ref[pl.ds(ks, SLC_QB), g * LANES:(g + 1) * LANES]
            member = _dot(sels[g], exp_ref[j])
            mask_add = (member - 1.0) * (-NEG_INF)
            s = _dot_nt(qs[g], kt) + bias_ref[d, g]
            s = (s.reshape(4, SLC_QB, SLC_QB) + mask_add[None]).reshape(4 * SLC_QB, SLC_QB)
            m_new = jnp.maximum(m, jnp.max(s, axis=1, keepdims=True))
            alpha = jnp.exp(m - m_new)
            e = jnp.exp(s - m_new)
            l = alpha * l + jnp.sum(e, axis=1, keepdims=True)
            acc = acc * alpha + _dot(e.astype(BF16), vt)
            out.append((m_new, l, acc))
        return tuple(out)

    init = tuple((jnp.full((4 * SLC_QB, 1), -3e38, F32), jnp.zeros((4 * SLC_QB, 1), F32),
                  jnp.zeros((4 * SLC_QB, LANES), F32)) for _ in range(N_KV))
    res = lax.fori_loop(0, i + 1, tile, init)
    for g in range(N_KV):
        _, l, acc = res[g]
        _unstack_heads(acc / l, o_ref, rows, g, lo, SLC_QB)


def _slc_expand_matrix():
    key_blk = (np.arange(N_SQB)[:, None, None] * SLC_QB + np.arange(SLC_QB)[None, None, :]) // SLC_LEN
    e = (np.arange(LANES)[None, :, None] == key_blk).astype(np.float32)
    return jnp.asarray(e, BF16)


def _selected_attention(q_all, k_all, v_all, sel, bias):
    n_tok = q_all.shape[0]
    return pl.pallas_call(
        _selected_kernel,
        grid=(n_tok // SLC_QB,),
        in_specs=[pl.BlockSpec((SLC_QB, 512), lambda gi: (gi, 1)),
                  pl.BlockSpec((SEQ, 256), lambda gi: (gi // N_SQB, 1)),
                  pl.BlockSpec((SEQ, 256), lambda gi: (gi // N_SQB, 1)),
                  pl.BlockSpec((SLC_QB, 256), lambda gi: (gi, 0)),
                  pl.BlockSpec((N_SQB, LANES, SLC_QB), lambda gi: (0, 0, 0)),
                  pl.BlockSpec((3, N_KV, 4 * SLC_QB, SLC_QB), lambda gi: (0, 0, 0, 0))],
        out_specs=pl.BlockSpec((SLC_QB, 512), lambda gi: (gi, 0)),
        out_shape=jax.ShapeDtypeStruct((n_tok, 512), BF16),
        compiler_params=_cparams(("arbitrary",)),
    )(q_all, k_all, v_all, sel, _slc_expand_matrix(), bias)


MIX_TM = 512


def _layer_norm(x, g, b):
    mu = jnp.mean(x, axis=-1, keepdims=True)
    xc = x - mu
    var = jnp.mean(xc * xc, axis=-1, keepdims=True)
    return xc * lax.rsqrt(var + LN_EPS) * g + b


def _mix_kernel(x_ref, ya_ref, oc_ref, os_ref, ow_ref, g_ref, m_ref, ge_ref, wa_ref, wb_ref,
                wo_ref, lng_ref, lnb_ref, wr_ref, br_ref, h_ref, hb_ref, gt_ref):
    gb = g_ref[...].astype(BF16)
    yb = (_dot(gb, ge_ref[0]) * oc_ref[...].astype(F32)
          + _dot(gb, ge_ref[1]) * os_ref[...].astype(F32)
          + _dot(gb, ge_ref[2]) * ow_ref[...].astype(F32))
    ma = _dot(ya_ref[...], wa_ref[...])
    mb = _dot(yb.astype(BF16), wb_ref[...])
    merged = m_ref[:, 0:D_MODEL].astype(F32) * ma + m_ref[:, D_MODEL:2 * D_MODEL].astype(F32) * mb
    mix = _dot(merged.astype(BF16), wo_ref[...])
    h = _layer_norm(DEEPNORM_ALPHA * x_ref[...] + mix, lng_ref[...], lnb_ref[...])
    h_ref[...] = h
    hb_ref[...] = h.astype(BF16)
    h_hi, h_lo = _split_bf16(h)
    logits = (_dot_nt(wr_ref[0], h_hi) + _dot_nt(wr_ref[0], h_lo) + _dot_nt(wr_ref[1], h_hi)
              + br_ref[...])
    row = lax.broadcasted_iota(jnp.int32, logits.shape, 0).astype(F32)
    vals, hits = [], []
    v = logits
    for _ in range(TOP_K):
        mx = jnp.max(v, axis=0, keepdims=True)
        idx = jnp.min(jnp.where(v == mx, row, 1e9), axis=0, keepdims=True)
        hit = row == idx
        vals.append(mx)
        hits.append(hit)
        v = jnp.where(hit, -3e38, v)
    es = [jnp.exp(t - vals[0]) for t in vals]
    den = es[0] + es[1] + es[2] + es[3]
    gt = jnp.zeros(logits.shape, F32)
    for k in range(TOP_K):
        gt = jnp.where(hits[k], es[k] / den, gt)
    gt_ref[...] = gt


def _gate_expand_matrix():
    e = np.zeros((3, LANES, 512), np.float32)
    for c in range(3):
        for h in range(N_HEADS):
            e[c, 3 * h + c, h * HEAD_DIM:(h + 1) * HEAD_DIM] = 1.0
    return jnp.asarray(e, BF16)


def _mix(x2d, ya, oc, osl, ow, gates, merge, w_branch_a, w_branch_b, w_out, ln_g, ln_b,
         w_router, b_router):
    n_tok = x2d.shape[0]
    tm = MIX_TM
    row = lambda i: (i, 0)
    fix2 = lambda i: (0, 0)
    fix3 = lambda i: (0, 0, 0)
    wr_t = w_router.T
    wr_hi = wr_t.astype(BF16)
    wr = jnp.stack([wr_hi, (wr_t - wr_hi.astype(F32)).astype(BF16)])
    return pl.pallas_call(
        _mix_kernel,
        grid=(n_tok // tm,),
        in_specs=[pl.BlockSpec((tm, D_MODEL), row),
                  pl.BlockSpec((tm, 512), row), pl.BlockSpec((tm, 512), row),
                  pl.BlockSpec((tm, 512), row), pl.BlockSpec((tm, 512), row),
                  pl.BlockSpec((tm, LANES), row), pl.BlockSpec((tm, 2 * D_MODEL), row),
                  pl.BlockSpec((3, LANES, 512), fix3),
                  pl.BlockSpec((512, D_MODEL), fix2), pl.BlockSpec((512, D_MODEL), fix2),
                  pl.BlockSpec((D_MODEL, D_MODEL), fix2),
                  pl.BlockSpec((1, D_MODEL), fix2), pl.BlockSpec((1, D_MODEL), fix2),
                  pl.BlockSpec((2, N_EXPERTS, D_MODEL), fix3),
                  pl.BlockSpec((N_EXPERTS, 1), fix2)],
        out_specs=[pl.BlockSpec((tm, D_MODEL), row), pl.BlockSpec((tm, D_MODEL), row),
                   pl.BlockSpec((N_EXPERTS, tm), lambda i: (0, i))],
        out_shape=[jax.ShapeDtypeStruct((n_tok, D_MODEL), F32),
                   jax.ShapeDtypeStruct((n_tok, D_MODEL), BF16),
                   jax.ShapeDtypeStruct((N_EXPERTS, n_tok), F32)],
        compiler_params=_cparams(("arbitrary",)),
    )(x2d, ya, oc, osl, ow, gates, merge, _gate_expand_matrix(),
      w_branch_a.astype(BF16), w_branch_b.astype(BF16), w_out.astype(BF16),
      ln_g[None, :], ln_b[None, :], wr, b_router[:, None])


MOE_SUB = 256


def _moe_kernel(gt_ref, x_ref, tri_ref, wgu_ref, bgu_ref, wd_ref, bd_ref, o_ref, pos_ref):
    e = pl.program_id(1)

    @pl.when(e == 0)
    def _():
        o_ref[...] = jnp.zeros(o_ref.shape, F32)
        carry = jnp.zeros((N_EXPERTS, 1), F32)
        for sb in range(MOE_TB // MOE_SUB):
            sl = slice(sb * MOE_SUB, (sb + 1) * MOE_SUB)
            hit = gt_ref[:, sl] > 0.0
            msk = jnp.where(hit, 1.0, 0.0)
            pos = _dot(msk.astype(BF16), tri_ref[...]) + carry
            pos_ref[:, sl] = jnp.where(hit, pos, -1.0)
            carry = carry + jnp.sum(msk, axis=1, keepdims=True)

    grow = gt_ref[pl.ds(e, 1), :]
    prow = pos_ref[pl.ds(e, 1), :]
    cnt = jnp.sum((grow > 0.0).astype(jnp.int32))
    n_chunks = (cnt + (MOE_ROWS - 1)) // MOE_ROWS
    rid = lax.broadcasted_iota(jnp.int32, (MOE_ROWS, 1), 0).astype(F32)

    def chunk(c, carry):
        onehot = prow == (rid + (c * MOE_ROWS).astype(F32))
        xc = _dot(jnp.where(onehot, 1.0, 0.0).astype(BF16), x_ref[...]).astype(BF16)
        hcat = _dot(xc, wgu_ref[0]) + bgu_ref[0]
        glu = jnp.minimum(hcat[:, 0:D_FF], SWIGLU_LIMIT)
        lin = jnp.clip(hcat[:, D_FF:2 * D_FF], -SWIGLU_LIMIT, SWIGLU_LIMIT)
        act = glu * jax.nn.sigmoid(SWIGLU_ALPHA * glu) * (lin + 1.0)
        y = _dot(act.astype(BF16), wd_ref[0]) + bd_ref[0]
        pw = jnp.where(onehot, grow, 0.0).astype(BF16)
        o_ref[...] += lax.dot_general(pw, y.astype(BF16), (((0,), (0,)), ((), ())),
                                      preferred_element_type=F32)
        return carry

    lax.fori_loop(0, n_chunks, chunk, 0)


def _moe(gt, hb, w_gate_up, b_gate_up, w_down, b_down):
    n_tok = hb.shape[0]
    tri = jnp.asarray(np.triu(np.ones((MOE_SUB, MOE_SUB), np.float32), k=1), BF16)
    return pl.pallas_call(
        _moe_kernel,
        grid=(n_tok // MOE_TB, N_EXPERTS),
        in_specs=[pl.BlockSpec((N_EXPERTS, MOE_TB), lambda t, e: (0, t)),
                  pl.BlockSpec((MOE_TB, D_MODEL), lambda t, e: (t, 0)),
                  pl.BlockSpec((MOE_SUB, MOE_SUB), lambda t, e: (0, 0)),
                  pl.BlockSpec((1, D_MODEL, 2 * D_FF), lambda t, e: (e, 0, 0)),
                  pl.BlockSpec((1, 1, 2 * D_FF), lambda t, e: (e, 0, 0)),
                  pl.BlockSpec((1, D_FF, D_MODEL), lambda t, e: (e, 0, 0)),
                  pl.BlockSpec((1, 1, D_MODEL), lambda t, e: (e, 0, 0))],
        out_specs=pl.BlockSpec((MOE_TB, D_MODEL), lambda t, e: (t, 0)),
        out_shape=jax.ShapeDtypeStruct((n_tok, D_MODEL), F32),
        scratch_shapes=[pltpu.VMEM((N_EXPERTS, MOE_TB), F32)],
        compiler_params=_cparams(("arbitrary", "arbitrary")),
    )(gt, hb, tri, w_gate_up.astype(BF16), b_gate_up[:, None, :], w_down.astype(BF16),
      b_down[:, None, :])


def _ln_kernel(h_ref, f_ref, g_ref, b_ref, o_ref):
    o_ref[...] = _layer_norm(DEEPNORM_ALPHA * h_ref[...] + f_ref[...], g_ref[...], b_ref[...])


def _final_ln(h, ffn, g, b):
    n_tok = h.shape[0]
    tm = 1024
    row = lambda i: (i, 0)
    fix = lambda i: (0, 0)
    return pl.pallas_call(
        _ln_kernel,
        grid=(n_tok // tm,),
        in_specs=[pl.BlockSpec((tm, D_MODEL), row), pl.BlockSpec((tm, D_MODEL), row),
                  pl.BlockSpec((1, D_MODEL), fix), pl.BlockSpec((1, D_MODEL), fix)],
        out_specs=pl.BlockSpec((tm, D_MODEL), row),
        out_shape=jax.ShapeDtypeStruct((n_tok, D_MODEL), F32),
        compiler_params=_cparams(("arbitrary",)),
    )(h, ffn, g[None, :], b[None, :])


def _layer(x2d, bsz, w_in, b_in, rel_bias, attn_sinks, cmp_pos_k, cmp_w1_k, cmp_w2_k, cmp_pos_v,
           cmp_w1_v, cmp_w2_v, w_branch_a, w_branch_b, w_out, ln1_g, ln1_b, w_router, b_router,
           w_gate_up, b_gate_up, w_down, b_down, ln2_g, ln2_b):
    w_packed, b_packed = _pack_in_weights(w_in, b_in)
    q_all, k_all, v_all, gates, merge, c_out = _in_projection(x2d, w_packed, b_packed)

    bias_a = rel_bias[:, :N_HEADS]
    bias_b = rel_bias[:, N_HEADS:]
    sinks = jnp.broadcast_to(attn_sinks.reshape(N_KV, 4, 1, 1), (N_KV, 4, QB, LANES))
    sinks = sinks.reshape(N_KV, 4 * QB, LANES)
    ya = _banded_attention(q_all, k_all, v_all, _band_bias(bias_a, A_WINDOW, 1), sinks,
                           q_col=0, kv_col=0, n_prev=1, n_sub=4)
    ow = _banded_attention(q_all, k_all, v_all, _band_bias(bias_b, B_WINDOW, 4), None,
                           q_col=1, kv_col=2, n_prev=4, n_sub=2)
    kv_cmp = _compress(c_out, bsz, cmp_pos_k, cmp_w1_k, cmp_w2_k, cmp_pos_v, cmp_w1_v, cmp_w2_v)
    oc, sel = _cmp_select(q_all, kv_cmp)
    osl = _selected_attention(q_all, k_all, v_all, sel, _slc_bias(bias_b))

    h, hb, gt = _mix(x2d, ya, oc, osl, ow, gates, merge, w_branch_a, w_branch_b, w_out,
                     ln1_g, ln1_b, w_router, b_router)
    ffn = _moe(gt, hb, w_gate_up, b_gate_up, w_down, b_down)
    return _final_ln(h, ffn, ln2_g, ln2_b)


def kernel(x, w_in, b_in, rel_bias, attn_sinks, cmp_pos_k, cmp_w1_k, cmp_w2_k, cmp_pos_v, cmp_w1_v,
           cmp_w2_v, w_branch_a, w_branch_b, w_out, ln1_g, ln1_b, w_router, b_router, w_gate_up,
           b_gate_up, w_down, b_down, ln2_g, ln2_b):
    bsz, seq, d = x.shape
    assert seq == SEQ and d == D_MODEL
    h = x.reshape(bsz * seq, d)
    for l in range(w_in.shape[0]):
        h = _layer(h, bsz, w_in[l], b_in[l], rel_bias, attn_sinks[l], cmp_pos_k[l], cmp_w1_k[l],
                   cmp_w2_k[l], cmp_pos_v[l], cmp_w1_v[l], cmp_w2_v[l], w_branch_a[l],
                   w_branch_b[l], w_out[l], ln1_g[l], ln1_b[l], w_router[l], b_router[l],
                   w_gate_up[l], b_gate_up[l], w_down[l], b_down[l], ln2_g[l], ln2_b[l])
    return h.reshape(bsz, seq, d)
```

```python
import functools
import math

import jax
import jax.numpy as jnp
import numpy as np
from jax import lax
from jax.experimental import pallas as pl
from jax.experimental.pallas import tpu as pltpu
from jax.experimental.pallas import tpu_sc as plsc

F32 = jnp.float32
BF16 = jnp.bfloat16

D_MODEL = 1024
SEQ = 2048
HEAD_DIM = 64
N_HEADS = 8
N_KV = 2
A_WINDOW = 128
B_WINDOW = 512
CMP_LEN = 32
CMP_STRIDE = 16
CMP_HIDDEN = 128
SLC_LEN = 64
SLC_TOP = 8
SLC_LOCAL = 2
N_BUCKETS = 32
REL_MAX_DIST = 128
N_EXPERTS = 32
TOP_K = 4
D_FF = D_MODEL
SWIGLU_LIMIT = 7.0
SWIGLU_ALPHA = 1.702
LN_EPS = 1e-5
NEG_INF = -1e30
FORCED_SCORE = 1e30
DEEPNORM_ALPHA = 2.0 ** 0.25
SCALE = HEAD_DIM ** -0.5

LANES = 128
QB = 128
N_QB = SEQ // QB
SLC_QB = 256
N_SQB = SEQ // SLC_QB
N_CHUNK = SEQ // CMP_STRIDE
N_SLC = SEQ // SLC_LEN
VMEM_LIMIT = 56 * 1024 * 1024

IN_OFF = dict(qa=0, ka=512, va=640, qb=768, kbc=1280, vbc=1408, kbs=1536, vbs=1664,
              kbw=1792, vbw=1920, gate=2048, merge=2072)
PROJ_TM = 512
MOE_BLOCK = 256
SC_WINDOW = 128


def _cparams(sem):
    return pltpu.CompilerParams(dimension_semantics=sem, vmem_limit_bytes=VMEM_LIMIT)


def _dot(a, b):
    return jnp.dot(a, b, preferred_element_type=F32)


def _dot_nt(a, b):
    return lax.dot_general(a, b, (((1,), (1,)), ((), ())), preferred_element_type=F32)


def _split_bf16(x):
    hi = x.astype(BF16)
    lo = (x - hi.astype(F32)).astype(BF16)
    return hi, lo


def _proj_kernel(x_ref, w_ref, b_ref, q_ref, k_ref, v_ref, g_ref, m_ref, c_ref):
    xb = x_ref[...].astype(BF16)

    def mm(c0, c1):
        return _dot(xb, w_ref[:, c0:c1]) + b_ref[:, c0:c1]

    for c in range(0, 1024, 512):
        q_ref[:, c:c + 512] = mm(c, c + 512).astype(BF16)
    k_ref[...] = mm(1024, 1792).astype(BF16)
    v_ref[...] = mm(1792, 2560).astype(BF16)
    g_ref[...] = jax.nn.sigmoid(mm(2560, 2688))
    for c in range(0, 2048, 512):
        m_ref[:, c:c + 512] = jax.nn.sigmoid(mm(2688 + c, 2688 + c + 512)).astype(BF16)
    c_ref[...] = mm(4736, 4992)


def _pack_in_weights(w_in, b_in):
    def cols(name, width):
        o = IN_OFF[name]
        return w_in[:, o:o + width], b_in[o:o + width]

    def dup_groups(name):
        w, b = cols(name, 128)
        ws, bs = [], []
        for g in range(N_KV):
            wg, bg = w[:, g * 64:(g + 1) * 64], b[g * 64:(g + 1) * 64]
            ws += [wg, wg]
            bs += [bg, bg]
        return jnp.concatenate(ws, axis=1), jnp.concatenate(bs)

    parts = [cols('qa', 512), cols('qb', 512),
             dup_groups('ka'), dup_groups('kbs'), dup_groups('kbw'),
             dup_groups('va'), dup_groups('vbs'), dup_groups('vbw')]
    wg, bg = cols('gate', 24)
    parts.append((jnp.pad(wg, ((0, 0), (0, 104))), jnp.pad(bg, (0, 104))))
    parts.append(cols('merge', 2048))
    parts.append(cols('kbc', 128))
    parts.append(cols('vbc', 128))
    w = jnp.concatenate([p[0] for p in parts], axis=1).astype(BF16)
    b = jnp.concatenate([p[1] for p in parts])[None, :]
    return w, b


def _in_projection(x2d, w_packed, b_packed):
    n_tok = x2d.shape[0]
    n_col = w_packed.shape[1]
    tm = PROJ_TM
    row = lambda i: (i, 0)
    fixed = lambda i: (0, 0)
    widths = (1024, 768, 768, 128, 2048, 256)
    dtypes = (BF16, BF16, BF16, F32, BF16, F32)
    return pl.pallas_call(
        _proj_kernel,
        grid=(n_tok // tm,),
        in_specs=[pl.BlockSpec((tm, D_MODEL), row),
                  pl.BlockSpec((D_MODEL, n_col), fixed),
                  pl.BlockSpec((1, n_col), fixed)],
        out_specs=[pl.BlockSpec((tm, w), row) for w in widths],
        out_shape=[jax.ShapeDtypeStruct((n_tok, w), dt) for w, dt in zip(widths, dtypes)],
        compiler_params=_cparams(("arbitrary",)),
    )(x2d, w_packed, b_packed)


def _bucket_np(rel):
    n = np.maximum(rel, 0)
    max_exact = N_BUCKETS // 2
    nf = np.maximum(n, 1).astype(np.float32)
    large = max_exact + (np.log(nf / max_exact) / math.log(REL_MAX_DIST / max_exact)
                         * (N_BUCKETS - max_exact)).astype(np.int32)
    large = np.minimum(large, N_BUCKETS - 1)
    return np.where(n < max_exact, n, large)


def _bias_tiles(bias_heads, rel, valid):
    n_var, rows, cols = rel.shape
    onehot = (jnp.asarray(_bucket_np(rel), jnp.int32)[..., None]
              == jnp.arange(N_BUCKETS, dtype=jnp.int32)).astype(F32)
    tab = jnp.einsum('vack,kh->vhac', onehot, bias_heads, precision=lax.Precision.HIGHEST)
    tab = jnp.where(jnp.asarray(valid)[:, None], tab, NEG_INF)
    return tab.reshape(n_var, N_KV, 4 * rows, cols)


def _band_bias(bias_heads, window, n_prev):
    w = (n_prev + 1) * QB
    v = np.arange(n_prev + 1)[:, None, None]
    rel = v * QB + np.arange(QB)[None, :, None] - np.arange(w)[None, None, :]
    return _bias_tiles(bias_heads, rel, (rel >= 0) & (rel < window))


def _slc_bias(bias_heads):
    d = np.arange(3)[:, None, None]
    rel = d * SLC_QB + np.arange(SLC_QB)[None, :, None] - np.arange(SLC_QB)[None, None, :]
    return _bias_tiles(bias_heads, rel, rel >= 0)


def _stack_heads(q_ref, rows, g, lo):
    parts = []
    for c in range(2):
        col = (2 * g + c) * LANES
        q2 = q_ref[rows, col:col + LANES] * SCALE
        parts += [jnp.where(lo, q2, 0), jnp.where(lo, 0, q2)]
    return jnp.concatenate(parts, axis=0)


def _unstack_heads(o, o_ref, rows, g, lo, n):
    for c in range(2):
        col = (2 * g + c) * LANES
        pair = jnp.where(lo, o[2 * c * n:(2 * c + 1) * n], o[(2 * c + 1) * n:(2 * c + 2) * n])
        o_ref[rows, col:col + LANES] = pair.astype(BF16)


def _banded_kernel(*refs, n_prev, n_sub, has_sinks):
    if has_sinks:
        q_ref, k_ref, v_ref, bias_ref, sink_ref, o_ref = refs
    else:
        q_ref, k_ref, v_ref, bias_ref, o_ref = refs
    w = (n_prev + 1) * QB
    lo = lax.broadcasted_iota(jnp.int32, (1, LANES), 1) < HEAD_DIM
    for u in range(n_sub):
        i = (pl.program_id(0) * n_sub + u) % N_QB
        start = pl.multiple_of(jnp.maximum(i - n_prev, 0) * QB, QB)
        var = jnp.minimum(i, n_prev)
        rows = slice(u * QB, (u + 1) * QB)
        for g in range(N_KV):
            kc = k_ref[pl.ds(start, w), g * LANES:(g + 1) * LANES]
            vc = v_ref[pl.ds(start, w), g * LANES:(g + 1) * LANES]
            s = _dot_nt(_stack_heads(q_ref, rows, g, lo), kc) + bias_ref[var, g]
            m = jnp.max(s, axis=1, keepdims=True)
            if has_sinks:
                sk = sink_ref[g][:, 0:1]
                m = jnp.maximum(m, sk)
            e = jnp.exp(s - m)
            den = jnp.sum(e, axis=1, keepdims=True)
            if has_sinks:
                den = den + jnp.exp(sk - m)
            o = _dot(e.astype(BF16), vc) / den
            _unstack_heads(o, o_ref, rows, g, lo, QB)


def _banded_attention(q_all, k_all, v_all, bias, sinks, q_col, kv_col, n_prev, n_sub):
    n_tok = q_all.shape[0]
    has_sinks = sinks is not None
    w = (n_prev + 1) * QB
    tq = QB * n_sub
    in_specs = [pl.BlockSpec((tq, 512), lambda gi: (gi, q_col)),
                pl.BlockSpec((SEQ, 256), lambda gi: (gi // (SEQ // tq), kv_col)),
                pl.BlockSpec((SEQ, 256), lambda gi: (gi // (SEQ // tq), kv_col)),
                pl.BlockSpec((n_prev + 1, N_KV, 4 * QB, w), lambda gi: (0, 0, 0, 0))]
    args = [q_all, k_all, v_all, bias]
    if has_sinks:
        in_specs.append(pl.BlockSpec((N_KV, 4 * QB, LANES), lambda gi: (0, 0, 0)))
        args.append(sinks)
    return pl.pallas_call(
        functools.partial(_banded_kernel, n_prev=n_prev, n_sub=n_sub, has_sinks=has_sinks),
        grid=(n_tok // tq,),
        in_specs=in_specs,
        out_specs=pl.BlockSpec((tq, 512), lambda gi: (gi, 0)),
        out_shape=jax.ShapeDtypeStruct((n_tok, 512), BF16),
        compiler_params=_cparams(("arbitrary",)),
    )(*args)


def _compress_kernel(z_ref, pos_ref, w1_ref, w2_ref, o_ref):
    half = CMP_STRIDE * HEAD_DIM
    for kv in range(2):
        for g in range(N_KV):
            z = z_ref[0, 2 * kv + g]
            za_h, za_l = _split_bf16(z + pos_ref[kv, :, 0:half])
            zb_h, zb_l = _split_bf16(z + pos_ref[kv, :, half:2 * half])
            w1a_h, w1a_l = w1_ref[kv, 0, 0:half], w1_ref[kv, 1, 0:half]
            w1b_h, w1b_l = w1_ref[kv, 0, half:2 * half], w1_ref[kv, 1, half:2 * half]
            ha = _dot(za_h, w1a_h) + _dot(za_l, w1a_h) + _dot(za_h, w1a_l)
            hb = _dot(zb_h, w1b_h) + _dot(zb_l, w1b_h) + _dot(zb_h, w1b_l)
            h = ha + pltpu.roll(hb, N_CHUNK - 1, axis=0)
            a = jax.nn.gelu(h)
            a_h, a_l = _split_bf16(a)
            out = (_dot(a_h, w2_ref[kv, 0]) + _dot(a_l, w2_ref[kv, 0]) + _dot(a_h, w2_ref[kv, 1]))
            o_ref[0, 2 * kv + g] = out.astype(BF16)


def _compress(c_out, bsz, cmp_pos_k, cmp_w1_k, cmp_w2_k, cmp_pos_v, cmp_w1_v, cmp_w2_v):
    z = c_out.reshape(bsz, N_CHUNK, CMP_STRIDE, 4, HEAD_DIM)
    z = jnp.transpose(z, (0, 3, 1, 2, 4)).reshape(bsz, 4, N_CHUNK, CMP_STRIDE * HEAD_DIM)
    pos = jnp.stack([cmp_pos_k.reshape(1, -1), cmp_pos_v.reshape(1, -1)])

    def split(w):
        hi = w.astype(BF16)
        return jnp.stack([hi, (w - hi.astype(F32)).astype(BF16)])

    w1 = jnp.stack([split(cmp_w1_k), split(cmp_w1_v)])
    w2 = jnp.stack([split(jnp.concatenate([cmp_w2_k, cmp_w2_k], axis=1)),
                    split(jnp.concatenate([cmp_w2_v, cmp_w2_v], axis=1))])
    return pl.pallas_call(
        _compress_kernel,
        grid=(bsz,),
        in_specs=[pl.BlockSpec((1, 4, N_CHUNK, 1024), lambda b: (b, 0, 0, 0)),
                  pl.BlockSpec((2, 1, 2048), lambda b: (0, 0, 0)),
                  pl.BlockSpec((2, 2, 2048, CMP_HIDDEN), lambda b: (0, 0, 0, 0)),
                  pl.BlockSpec((2, 2, CMP_HIDDEN, LANES), lambda b: (0, 0, 0, 0))],
        out_specs=pl.BlockSpec((1, 4, N_CHUNK, LANES), lambda b: (b, 0, 0, 0)),
        out_shape=jax.ShapeDtypeStruct((bsz, 4, N_CHUNK, LANES), BF16),
        compiler_params=_cparams(("arbitrary",)),
    )(z, pos, w1, w2)


CMP_QB = 256


def _cmp_select_kernel(q_ref, kv_ref, ov_ref, o_ref, sel_ref):
    i = pl.program_id(0) % (SEQ // CMP_QB)
    lane = lax.broadcasted_iota(jnp.int32, (1, LANES), 1)
    lo = lane < HEAD_DIM
    t = i * CMP_QB + lax.broadcasted_iota(jnp.int32, (CMP_QB, 1), 0)
    valid = (lane * CMP_STRIDE + (CMP_LEN - 1)) <= t
    any_valid = t >= CMP_LEN - 1
    cur = lax.shift_right_logical(t, int(math.log2(SLC_LEN)))
    forced = (lane == 0) | ((lane <= cur) & (lane > cur - SLC_LOCAL))
    future = lane > cur
    lane_f = lane.astype(F32)
    for g in range(N_KV):
        kc = kv_ref[0, g]
        vc = kv_ref[0, 2 + g]
        v_half = (jnp.where(lo, vc, 0), jnp.where(lo, 0, vc))
        psum = jnp.zeros((CMP_QB, LANES), F32)
        for c in range(2):
            col = (2 * g + c) * LANES
            q2 = q_ref[:, col:col + LANES] * SCALE
            acc = jnp.zeros((CMP_QB, LANES), F32)
            for hh in range(2):
                qm = jnp.where(lo, q2, 0) if hh == 0 else jnp.where(lo, 0, q2)
                s = jnp.where(valid, _dot_nt(qm, kc), NEG_INF)
                m = jnp.max(s, axis=1, keepdims=True)
                e = jnp.exp(s - m)
                p = e / jnp.sum(e, axis=1, keepdims=True)
                p = jnp.where(any_valid, p, 0.0)
                psum = psum + p
                acc = acc + _dot(p.astype(BF16), v_half[hh])
            o_ref[:, col:col + LANES] = acc.astype(BF16)
        p_h, p_l = _split_bf16(psum)
        imp = _dot(p_h, ov_ref[...]) + _dot(p_l, ov_ref[...])
        imp = jnp.where(forced, FORCED_SCORE, imp)
        imp = jnp.where(future, NEG_INF, imp)
        imp = jnp.where(lane < N_SLC, imp, -3e38)
        sel = jnp.zeros((CMP_QB, LANES), F32)
        for _ in range(SLC_TOP):
            mx = jnp.max(imp, axis=1, keepdims=True)
            idx = jnp.min(jnp.where(imp == mx, lane_f, 1e9), axis=1, keepdims=True)
            hit = lane_f == idx
            sel = jnp.where(hit, 1.0, sel)
            imp = jnp.where(hit, -3e38, imp)
        sel_ref[:, g * LANES:(g + 1) * LANES] = sel.astype(BF16)


def _cmp_overlap_matrix():
    nc = (SEQ - CMP_LEN) // CMP_STRIDE + 1
    cs = np.arange(nc)[:, None] * CMP_STRIDE
    ss = np.arange(N_SLC)[None, :] * SLC_LEN
    ov = np.clip(np.minimum(cs + CMP_LEN, ss + SLC_LEN) - np.maximum(cs, ss), 0, None)
    out = np.zeros((LANES, LANES), np.float32)
    out[:nc, :N_SLC] = ov / CMP_LEN
    return jnp.asarray(out, BF16)


def _cmp_select(q_all, kv_cmp):
    n_tok = q_all.shape[0]
    nqb = SEQ // CMP_QB
    return pl.pallas_call(
        _cmp_select_kernel,
        grid=(n_tok // CMP_QB,),
        in_specs=[pl.BlockSpec((CMP_QB, 512), lambda gi: (gi, 1)),
                  pl.BlockSpec((1, 4, N_CHUNK, LANES), lambda gi: (gi // nqb, 0, 0, 0)),
                  pl.BlockSpec((LANES, LANES), lambda gi: (0, 0))],
        out_specs=[pl.BlockSpec((CMP_QB, 512), lambda gi: (gi, 0)),
                   pl.BlockSpec((CMP_QB, 256), lambda gi: (gi, 0))],
        out_shape=[jax.ShapeDtypeStruct((n_tok, 512), BF16),
                   jax.ShapeDtypeStruct((n_tok, 256), BF16)],
        compiler_params=_cparams(("arbitrary",)),
    )(q_all, kv_cmp, _cmp_overlap_matrix())


def _selected_kernel(q_ref, k_ref, v_ref, sel_ref, exp_ref, bias_ref, o_ref):
    i = pl.program_id(0) % N_SQB
    lo = lax.broadcasted_iota(jnp.int32, (1, LANES), 1) < HEAD_DIM
    rows = slice(0, SLC_QB)
    qs = [_stack_heads(q_ref, rows, g, lo) for g in range(N_KV)]
    sels = [sel_ref[:, g * LANES:(g + 1) * LANES] for g in range(N_KV)]

    def tile(j, carry):
        ks = pl.multiple_of(j * SLC_QB, SLC_QB)
        d = jnp.minimum(i - j, 2)
        out = []
        for g in range(N_KV):
            m, l, acc = carry[g]
            kt = k_ref[pl.ds(ks, SLC_QB), g * LANES:(g + 1) * LANES]
            vt = v_ref[pl.ds(ks, SLC_QB), g * LANES:(g + 1) * LANES]
            member = _dot(sels[g], exp_ref[j])
            mask_add = (member - 1.0) * (-NEG_INF)
            s = _dot_nt(qs[g], kt) + bias_ref[d, g]
            s = (s.reshape(4, SLC_QB, SLC_QB) + mask_add[None]).reshape(4 * SLC_QB, SLC_QB)
            m_new = jnp.maximum(m, jnp.max(s, axis=1, keepdims=True))
            alpha = jnp.exp(m - m_new)
            e = jnp.exp(s - m_new)
            l = alpha * l + jnp.sum(e, axis=1, keepdims=True)
            acc = acc * alpha + _dot(e.astype(BF16), vt)
            out.append((m_new, l, acc))
        return tuple(out)

    init = tuple((jnp.full((4 * SLC_QB, 1), -3e38, F32), jnp.zeros((4 * SLC_QB, 1), F32),
                  jnp.zeros((4 * SLC_QB, LANES), F32)) for _ in range(N_KV))
    res = lax.fori_loop(0, i + 1, tile, init)
    for g in range(N_KV):
        _, l, acc = res[g]
        _unstack_heads(acc / l, o_ref, rows, g, lo, SLC_QB)


def _slc_expand_matrix():
    key_blk = (np.arange(N_SQB)[:, None, None] * SLC_QB + np.arange(SLC_QB)[None, None, :]) // SLC_LEN
    e = (np.arange(LANES)[None, :, None] == key_blk).astype(np.float32)
    return jnp.asarray(e, BF16)


def _selected_attention(q_all, k_all, v_all, sel, bias):
    n_tok = q_all.shape[0]
    return pl.pallas_call(
        _selected_kernel,
        grid=(n_tok // SLC_QB,),
        in_specs=[pl.BlockSpec((SLC_QB, 512), lambda gi: (gi, 1)),
                  pl.BlockSpec((SEQ, 256), lambda gi: (gi // N_SQB, 1)),
                  pl.BlockSpec((SEQ, 256), lambda gi: (gi // N_SQB, 1)),
                  pl.BlockSpec((SLC_QB, 256), lambda gi: (gi, 0)),
                  pl.BlockSpec((N_SQB, LANES, SLC_QB), lambda gi: (0, 0, 0)),
                  pl.BlockSpec((3, N_KV, 4 * SLC_QB, SLC_QB), lambda gi: (0, 0, 0, 0))],
        out_specs=pl.BlockSpec((SLC_QB, 512), lambda gi: (gi, 0)),
        out_shape=jax.ShapeDtypeStruct((n_tok, 512), BF16),
        compiler_params=_cparams(("arbitrary",)),
    )(q_all, k_all, v_all, sel, _slc_expand_matrix(), bias)


MIX_TM = 512


def _layer_norm(x, g, b):
    mu = jnp.mean(x, axis=-1, keepdims=True)
    xc = x - mu
    var = jnp.mean(xc * xc, axis=-1, keepdims=True)
    return xc * lax.rsqrt(var + LN_EPS) * g + b


def _pack_bf16_pair(a, b):
    ia = lax.bitcast_convert_type(a.astype(BF16).astype(F32), jnp.int32)
    ib = lax.bitcast_convert_type(b.astype(BF16).astype(F32), jnp.int32)
    return lax.shift_right_logical(ia, 16) | (ib & -65536)


def _unpack_bf16_pair(w):
    a = lax.bitcast_convert_type(lax.shift_left(w, 16), F32)
    b = lax.bitcast_convert_type(w & -65536, F32)
    return a, b


def _unpack_row(w_ref, k=None):
    parts = []
    for half in range(2):
        parts += list(_unpack_bf16_pair(w_ref[half] if k is None else w_ref[half, k]))
    return jnp.concatenate(parts, axis=1)


def _mix_kernel(x_ref, ya_ref, oc_ref, os_ref, ow_ref, g_ref, m_ref, ge_ref, wa_ref, wb_ref,
                wo_ref, lng_ref, lnb_ref, wr_ref, br_ref, tri_ref,
                h_ref, hw_ref, e_ref, r_ref, g_out_ref, cnt_ref):
    gb = g_ref[...].astype(BF16)
    yb = (_dot(gb, ge_ref[0]) * oc_ref[...].astype(F32)
          + _dot(gb, ge_ref[1]) * os_ref[...].astype(F32)
          + _dot(gb, ge_ref[2]) * ow_ref[...].astype(F32))
    ma = _dot(ya_ref[...], wa_ref[...])
    mb = _dot(yb.astype(BF16), wb_ref[...])
    merged = m_ref[:, 0:D_MODEL].astype(F32) * ma + m_ref[:, D_MODEL:2 * D_MODEL].astype(F32) * mb
    mix = _dot(merged.astype(BF16), wo_ref[...])
    h = _layer_norm(DEEPNORM_ALPHA * x_ref[...] + mix, lng_ref[...], lnb_ref[...])
    h_ref[...] = h
    hw_ref[0] = _pack_bf16_pair(h[:, 0:256], h[:, 256:512])
    hw_ref[1] = _pack_bf16_pair(h[:, 512:768], h[:, 768:1024])
    h_hi, h_lo = _split_bf16(h)
    logits = (_dot_nt(wr_ref[0], h_hi) + _dot_nt(wr_ref[0], h_lo) + _dot_nt(wr_ref[1], h_hi)
              + br_ref[...])
    row = lax.broadcasted_iota(jnp.int32, logits.shape, 0).astype(F32)
    vals, hits, idxs = [], [], []
    v = logits
    for _ in range(TOP_K):
        mx = jnp.max(v, axis=0, keepdims=True)
        idx = jnp.min(jnp.where(v == mx, row, 1e9), axis=0, keepdims=True)
        hit = row == idx
        vals.append(mx)
        hits.append(hit)
        idxs.append(idx)
        v = jnp.where(hit, -3e38, v)
    es = [jnp.exp(t - vals[0]) for t in vals]
    den = es[0] + es[1] + es[2] + es[3]

    @pl.when(pl.program_id(0) == 0)
    def _():
        cnt_ref[...] = jnp.zeros(cnt_ref.shape, F32)

    routed = sum(jnp.where(hit, 1.0, 0.0) for hit in hits)
    before = _dot(routed.astype(BF16), tri_ref[...]) + cnt_ref[:, 0:1]
    for k in range(TOP_K):
        e_ref[k:k + 1, :] = idxs[k].astype(jnp.int32)
        r_ref[k:k + 1, :] = jnp.sum(jnp.where(hits[k], before, 0.0), axis=0,
                                    keepdims=True).astype(jnp.int32)
        g_out_ref[k:k + 1, :] = es[k] / den
    cnt_ref[...] = cnt_ref[...] + jnp.sum(routed, axis=1, keepdims=True)


def _gate_expand_matrix():
    e = np.zeros((3, LANES, 512), np.float32)
    for c in range(3):
        for h in range(N_HEADS):
            e[c, 3 * h + c, h * HEAD_DIM:(h + 1) * HEAD_DIM] = 1.0
    return jnp.asarray(e, BF16)


def _mix(x2d, ya, oc, osl, ow, gates, merge, w_branch_a, w_branch_b, w_out, ln_g, ln_b,
         w_router, b_router):
    n_tok = x2d.shape[0]
    tm = MIX_TM
    row = lambda i: (i, 0)
    fix2 = lambda i: (0, 0)
    fix3 = lambda i: (0, 0, 0)
    wr_t = w_router.T
    wr_hi = wr_t.astype(BF16)
    wr = jnp.stack([wr_hi, (wr_t - wr_hi.astype(F32)).astype(BF16)])
    tri = jnp.asarray(np.triu(np.ones((tm, tm), np.float32), k=1), BF16)
    tok = lambda i: (0, i)
    return pl.pallas_call(
        _mix_kernel,
        grid=(n_tok // tm,),
        in_specs=[pl.BlockSpec((tm, D_MODEL), row),
                  pl.BlockSpec((tm, 512), row), pl.BlockSpec((tm, 512), row),
                  pl.BlockSpec((tm, 512), row), pl.BlockSpec((tm, 512), row),
                  pl.BlockSpec((tm, LANES), row), pl.BlockSpec((tm, 2 * D_MODEL), row),
                  pl.BlockSpec((3, LANES, 512), fix3),
                  pl.BlockSpec((512, D_MODEL), fix2), pl.BlockSpec((512, D_MODEL), fix2),
                  pl.BlockSpec((D_MODEL, D_MODEL), fix2),
                  pl.BlockSpec((1, D_MODEL), fix2), pl.BlockSpec((1, D_MODEL), fix2),
                  pl.BlockSpec((2, N_EXPERTS, D_MODEL), fix3),
                  pl.BlockSpec((N_EXPERTS, 1), fix2),
                  pl.BlockSpec((tm, tm), fix2)],
        out_specs=[pl.BlockSpec((tm, D_MODEL), row),
                   pl.BlockSpec((2, tm, 256), lambda i: (0, i, 0)),
                   pl.BlockSpec((TOP_K, tm), tok), pl.BlockSpec((TOP_K, tm), tok),
                   pl.BlockSpec((TOP_K, tm), tok),
                   pl.BlockSpec((N_EXPERTS, LANES), fix2)],
        out_shape=[jax.ShapeDtypeStruct((n_tok, D_MODEL), F32),
                   jax.ShapeDtypeStruct((2, n_tok, 256), jnp.int32),
                   jax.ShapeDtypeStruct((TOP_K, n_tok), jnp.int32),
                   jax.ShapeDtypeStruct((TOP_K, n_tok), jnp.int32),
                   jax.ShapeDtypeStruct((TOP_K, n_tok), F32),
                   jax.ShapeDtypeStruct((N_EXPERTS, LANES), F32)],
        compiler_params=_cparams(("arbitrary",)),
    )(x2d, ya, oc, osl, ow, gates, merge, _gate_expand_matrix(),
      w_branch_a.astype(BF16), w_branch_b.astype(BF16), w_out.astype(BF16),
      ln_g[None, :], ln_b[None, :], wr, b_router[:, None], tri)


def _sc_mesh():
    return plsc.VectorSubcoreMesh(core_axis_name="core", subcore_axis_name="subcore")


def _sc_scatter_rows(x, idx, n_out):
    n_tok = x.shape[1]
    n_win = n_tok // SC_WINDOW
    x2 = x.reshape(2 * n_tok, 256)

    @pl.kernel(out_type=jax.ShapeDtypeStruct((n_out, 256), x.dtype), mesh=_sc_mesh(),
               scratch_types=[])
    def scatter(x_hbm, i_hbm, o_hbm):
        def body(x_vmem, i_vmem):
            pltpu.sync_copy(x_vmem, o_hbm.at[i_vmem.at[0]])

        pltpu.emit_pipeline(
            body, grid=(2 * TOP_K * n_win,),
            in_specs=[pl.BlockSpec((SC_WINDOW, 256),
                                   lambda c: ((c // (TOP_K * n_win)) * n_win + c % n_win, 0)),
                      pl.BlockSpec((1, SC_WINDOW), lambda c: (0, c))],
            out_specs=[], core_axis_name=("core", "subcore"),
            dimension_semantics=(pltpu.PARALLEL,))(x_hbm, i_hbm)

    return scatter(x2, idx)


def _sc_gather_rows(y, idx):
    n_idx = idx.shape[1]

    @pl.kernel(out_type=jax.ShapeDtypeStruct((n_idx, 256), y.dtype), mesh=_sc_mesh())
    def gather(y_hbm, i_hbm, o_hbm):
        def body(i_vmem, o_vmem):
            pltpu.sync_copy(y_hbm.at[i_vmem.at[0]], o_vmem)

        pltpu.emit_pipeline(
            body, grid=(n_idx // SC_WINDOW,),
            in_specs=[pl.BlockSpec((1, SC_WINDOW), lambda c: (0, c))],
            out_specs=[pl.BlockSpec((SC_WINDOW, 256), lambda c: (c, 0))],
            core_axis_name=("core", "subcore"),
            dimension_semantics=(pltpu.PARALLEL,))(i_hbm, o_hbm)

    return gather(y, idx)


def _ffn_kernel(be_ref, nu_ref, r_ref, wgu_ref, bgu_ref, wd_ref, bd_ref, y_ref):
    @pl.when(pl.program_id(0) < nu_ref[0])
    def _():
        x = _unpack_row(r_ref).astype(BF16)
        hcat = _dot(x, wgu_ref[0]) + bgu_ref[0]
        glu = jnp.minimum(hcat[:, 0:D_FF], SWIGLU_LIMIT)
        lin = jnp.clip(hcat[:, D_FF:2 * D_FF], -SWIGLU_LIMIT, SWIGLU_LIMIT)
        act = glu * jax.nn.sigmoid(SWIGLU_ALPHA * glu) * (lin + 1.0)
        y = _dot(act.astype(BF16), wd_ref[0]) + bd_ref[0]
        y_ref[0] = _pack_bf16_pair(y[:, 0:256], y[:, 256:512])
        y_ref[1] = _pack_bf16_pair(y[:, 512:768], y[:, 768:1024])


def _expert_ffn(rows, block_expert, n_used, w_gate_up, b_gate_up, w_down, b_down):
    n_slots = rows.shape[1]
    grid_spec = pltpu.PrefetchScalarGridSpec(
        num_scalar_prefetch=2,
        grid=(n_slots // MOE_BLOCK,),
        in_specs=[pl.BlockSpec((2, MOE_BLOCK, 256), lambda i, be, nu: (0, i, 0)),
                  pl.BlockSpec((1, D_MODEL, 2 * D_FF), lambda i, be, nu: (be[i], 0, 0)),
                  pl.BlockSpec((1, 1, 2 * D_FF), lambda i, be, nu: (be[i], 0, 0)),
                  pl.BlockSpec((1, D_FF, D_MODEL), lambda i, be, nu: (be[i], 0, 0)),
                  pl.BlockSpec((1, 1, D_MODEL), lambda i, be, nu: (be[i], 0, 0))],
        out_specs=pl.BlockSpec((2, MOE_BLOCK, 256), lambda i, be, nu: (0, i, 0)))
    return pl.pallas_call(
        _ffn_kernel,
        grid_spec=grid_spec,
        out_shape=jax.ShapeDtypeStruct((2, n_slots, 256), jnp.int32),
        compiler_params=_cparams(("arbitrary",)),
    )(block_expert, n_used, rows, w_gate_up.astype(BF16), b_gate_up[:, None, :],
      w_down.astype(BF16), b_down[:, None, :])


def _dispatch_plan(eidx, rank, counts, n_tok):
    n_slots = n_tok * TOP_K + N_EXPERTS * MOE_BLOCK
    cnt = counts[:, 0].astype(jnp.int32)
    padded = (cnt + (MOE_BLOCK - 1)) // MOE_BLOCK * MOE_BLOCK
    ends = jnp.cumsum(padded)
    base = ends - padded
    onehot = eidx[..., None] == jnp.arange(N_EXPERTS, dtype=jnp.int32)
    slot = jnp.sum(jnp.where(onehot, base, 0), axis=-1) + rank
    blk_start = jnp.arange(n_slots // MOE_BLOCK, dtype=jnp.int32) * MOE_BLOCK
    block_expert = jnp.minimum(jnp.searchsorted(ends, blk_start, side='right'),
                               N_EXPERTS - 1).astype(jnp.int32)
    n_used = (ends[-1:] // MOE_BLOCK).astype(jnp.int32)
    idx = jnp.concatenate([slot, slot + n_slots], axis=0).reshape(1, 2 * TOP_K * n_tok)
    return n_slots, idx, block_expert, n_used


def _moe(hw, eidx, rank, counts, w_gate_up, b_gate_up, w_down, b_down):
    n_tok = hw.shape[1]
    n_slots, idx, block_expert, n_used = _dispatch_plan(eidx, rank, counts, n_tok)
    rows = _sc_scatter_rows(hw, idx, 2 * n_slots).reshape(2, n_slots, 256)
    y = _expert_ffn(rows, block_expert, n_used, w_gate_up, b_gate_up, w_down, b_down)
    yg = _sc_gather_rows(y.reshape(2 * n_slots, 256), idx)
    return yg.reshape(2, TOP_K, n_tok, 256)


def _combine_ln_kernel(h_ref, yg_ref, w_ref, g_ref, b_ref, o_ref):
    ffn = jnp.zeros(h_ref.shape, F32)
    for k in range(TOP_K):
        ffn = ffn + w_ref[:, k:k + 1] * _unpack_row(yg_ref, k)
    o_ref[...] = _layer_norm(DEEPNORM_ALPHA * h_ref[...] + ffn, g_ref[...], b_ref[...])


def _combine_ln(h, yg, w_tok, g, b):
    n_tok = h.shape[0]
    tm = 512
    row = lambda i: (i, 0)
    fix = lambda i: (0, 0)
    return pl.pallas_call(
        _combine_ln_kernel,
        grid=(n_tok // tm,),
        in_specs=[pl.BlockSpec((tm, D_MODEL), row),
                  pl.BlockSpec((2, TOP_K, tm, 256), lambda i: (0, 0, i, 0)),
                  pl.BlockSpec((tm, TOP_K), row),
                  pl.BlockSpec((1, D_MODEL), fix), pl.BlockSpec((1, D_MODEL), fix)],
        out_specs=pl.BlockSpec((tm, D_MODEL), row),
        out_shape=jax.ShapeDtypeStruct((n_tok, D_MODEL), F32),
        compiler_params=_cparams(("arbitrary",)),
    )(h, yg, w_tok, g[None, :], b[None, :])


def _layer(x2d, bsz, w_in, b_in, rel_bias, attn_sinks, cmp_pos_k, cmp_w1_k, cmp_w2_k, cmp_pos_v,
           cmp_w1_v, cmp_w2_v, w_branch_a, w_branch_b, w_out, ln1_g, ln1_b, w_router, b_router,
           w_gate_up, b_gate_up, w_down, b_down, ln2_g, ln2_b):
    w_packed, b_packed = _pack_in_weights(w_in, b_in)
    q_all, k_all, v_all, gates, merge, c_out = _in_projection(x2d, w_packed, b_packed)

    bias_a = rel_bias[:, :N_HEADS]
    bias_b = rel_bias[:, N_HEADS:]
    sinks = jnp.broadcast_to(attn_sinks.reshape(N_KV, 4, 1, 1), (N_KV, 4, QB, LANES))
    sinks = sinks.reshape(N_KV, 4 * QB, LANES)
    ya = _banded_attention(q_all, k_all, v_all, _band_bias(bias_a, A_WINDOW, 1), sinks,
                           q_col=0, kv_col=0, n_prev=1, n_sub=4)
    ow = _banded_attention(q_all, k_all, v_all, _band_bias(bias_b, B_WINDOW, 4), None,
                           q_col=1, kv_col=2, n_prev=4, n_sub=2)
    kv_cmp = _compress(c_out, bsz, cmp_pos_k, cmp_w1_k, cmp_w2_k, cmp_pos_v, cmp_w1_v, cmp_w2_v)
    oc, sel = _cmp_select(q_all, kv_cmp)
    osl = _selected_attention(q_all, k_all, v_all, sel, _slc_bias(bias_b))

    h, hw, eidx, rank, gate_w, counts = _mix(x2d, ya, oc, osl, ow, gates, merge, w_branch_a,
                                             w_branch_b, w_out, ln1_g, ln1_b, w_router, b_router)
    yg = _moe(hw, eidx, rank, counts, w_gate_up, b_gate_up, w_down, b_down)
    return _combine_ln(h, yg, gate_w.T, ln2_g, ln2_b)


def kernel(x, w_in, b_in, rel_bias, attn_sinks, cmp_pos_k, cmp_w1_k, cmp_w2_k, cmp_pos_v, cmp_w1_v,
           cmp_w2_v, w_branch_a, w_branch_b, w_out, ln1_g, ln1_b, w_router, b_router, w_gate_up,
           b_gate_up, w_down, b_down, ln2_g, ln2_b):
    bsz, seq, d = x.shape
    assert seq == SEQ and d == D_MODEL
    h = x.reshape(bsz * seq, d)
    for l in range(w_in.shape[0]):
        h = _layer(h, bsz, w_in[l], b_in[l], rel_bias, attn_sinks[l], cmp_pos_k[l], cmp_w1_k[l],
                   cmp_w2_k[l], cmp_pos_v[l], cmp_w1_v[l], cmp_w2_v[l], w_branch_a[l],
                   w_branch_b[l], w_out[l], ln1_g[l], ln1_b[l], w_router[l], b_router[l],
                   w_gate_up[l], b_gate_up[l], w_down[l], b_down[l], ln2_g[l], ln2_b[l])
    return h.reshape(bsz, seq, d)
```

```python
import functools
import math

import jax
import jax.numpy as jnp
import numpy as np
from jax import lax
from jax.experimental import pallas as pl
from jax.experimental.pallas import tpu as pltpu
from jax.experimental.pallas import tpu_sc as plsc

F32 = jnp.float32
BF16 = jnp.bfloat16

D_MODEL = 1024
SEQ = 2048
HEAD_DIM = 64
N_HEADS = 8
N_KV = 2
A_WINDOW = 128
B_WINDOW = 512
CMP_LEN = 32
CMP_STRIDE = 16
CMP_HIDDEN = 128
SLC_LEN = 64
SLC_TOP = 8
SLC_LOCAL = 2
N_BUCKETS = 32
REL_MAX_DIST = 128
N_EXPERTS = 32
TOP_K = 4
D_FF = D_MODEL
SWIGLU_LIMIT = 7.0
SWIGLU_ALPHA = 1.702
LN_EPS = 1e-5
NEG_INF = -1e30
FORCED_SCORE = 1e30
DEEPNORM_ALPHA = 2.0 ** 0.25
SCALE = HEAD_DIM ** -0.5

LANES = 128
QB = 128
N_QB = SEQ // QB
SLC_QB = 256
N_SQB = SEQ // SLC_QB
N_CHUNK = SEQ // CMP_STRIDE
N_SLC = SEQ // SLC_LEN
VMEM_LIMIT = 56 * 1024 * 1024

IN_OFF = dict(qa=0, ka=512, va=640, qb=768, kbc=1280, vbc=1408, kbs=1536, vbs=1664,
              kbw=1792, vbw=1920, gate=2048, merge=2072)
PROJ_TM = 512
MOE_BLOCK = 512
SC_WINDOW = 128


def _cparams(sem):
    return pltpu.CompilerParams(dimension_semantics=sem, vmem_limit_bytes=VMEM_LIMIT)


def _dot(a, b):
    return jnp.dot(a, b, preferred_element_type=F32)


def _dot_nt(a, b):
    return lax.dot_general(a, b, (((1,), (1,)), ((), ())), preferred_element_type=F32)


def _dot_tn(a, b):
    return lax.dot_general(a, b, (((0,), (0,)), ((), ())), preferred_element_type=F32)


def _split_bf16(x):
    hi = x.astype(BF16)
    lo = (x - hi.astype(F32)).astype(BF16)
    return hi, lo


def _proj_kernel(x_ref, w_ref, b_ref, q_ref, k_ref, v_ref, g_ref, m_ref, c_ref):
    xb = x_ref[...].astype(BF16)

    def mm(c0, c1):
        return _dot(xb, w_ref[:, c0:c1]) + b_ref[:, c0:c1]

    for c in range(0, 1024, 512):
        q_ref[:, c:c + 512] = mm(c, c + 512).astype(BF16)
    k_ref[...] = mm(1024, 1792).astype(BF16)
    v_ref[...] = mm(1792, 2560).astype(BF16)
    g_ref[...] = jax.nn.sigmoid(mm(2560, 2688))
    for c in range(0, 2048, 512):
        m_ref[:, c:c + 512] = jax.nn.sigmoid(mm(2688 + c, 2688 + c + 512)).astype(BF16)
    c_ref[...] = mm(4736, 4992)


def _pack_in_weights(w_in, b_in):
    def cols(name, width):
        o = IN_OFF[name]
        return w_in[:, o:o + width], b_in[o:o + width]

    def dup_groups(name):
        w, b = cols(name, 128)
        ws, bs = [], []
        for g in range(N_KV):
            wg, bg = w[:, g * 64:(g + 1) * 64], b[g * 64:(g + 1) * 64]
            ws += [wg, wg]
            bs += [bg, bg]
        return jnp.concatenate(ws, axis=1), jnp.concatenate(bs)

    parts = [cols('qa', 512), cols('qb', 512),
             dup_groups('ka'), dup_groups('kbs'), dup_groups('kbw'),
             dup_groups('va'), dup_groups('vbs'), dup_groups('vbw')]
    wg, bg = cols('gate', 24)
    parts.append((jnp.pad(wg, ((0, 0), (0, 104))), jnp.pad(bg, (0, 104))))
    parts.append(cols('merge', 2048))
    parts.append(cols('kbc', 128))
    parts.append(cols('vbc', 128))
    w = jnp.concatenate([p[0] for p in parts], axis=1).astype(BF16)
    b = jnp.concatenate([p[1] for p in parts])[None, :]
    return w, b


def _in_projection(x2d, w_packed, b_packed):
    n_tok = x2d.shape[0]
    n_col = w_packed.shape[1]
    tm = PROJ_TM
    row = lambda i: (i, 0)
    fixed = lambda i: (0, 0)
    widths = (1024, 768, 768, 128, 2048, 256)
    dtypes = (BF16, BF16, BF16, F32, BF16, F32)
    return pl.pallas_call(
        _proj_kernel,
        grid=(n_tok // tm,),
        in_specs=[pl.BlockSpec((tm, D_MODEL), row),
                  pl.BlockSpec((D_MODEL, n_col), fixed),
                  pl.BlockSpec((1, n_col), fixed)],
        out_specs=[pl.BlockSpec((tm, w), row) for w in widths],
        out_shape=[jax.ShapeDtypeStruct((n_tok, w), dt) for w, dt in zip(widths, dtypes)],
        compiler_params=_cparams(("arbitrary",)),
    )(x2d, w_packed, b_packed)


def _bucket_np(rel):
    n = np.maximum(rel, 0)
    max_exact = N_BUCKETS // 2
    nf = np.maximum(n, 1).astype(np.float32)
    large = max_exact + (np.log(nf / max_exact) / math.log(REL_MAX_DIST / max_exact)
                         * (N_BUCKETS - max_exact)).astype(np.int32)
    large = np.minimum(large, N_BUCKETS - 1)
    return np.where(n < max_exact, n, large)


def _bias_tiles(bias_heads, rel, valid):
    n_var, n_q, n_k = rel.shape
    onehot = (jnp.asarray(_bucket_np(rel), jnp.int32)[..., None]
              == jnp.arange(N_BUCKETS, dtype=jnp.int32)).astype(F32)
    tab = jnp.einsum('vack,kh->vhca', onehot, bias_heads, precision=lax.Precision.HIGHEST)
    tab = jnp.where(jnp.asarray(np.swapaxes(valid, 1, 2))[:, None], tab, NEG_INF)
    tab = tab.reshape(n_var, N_KV, 4, n_k, n_q).transpose(0, 1, 3, 2, 4)
    return tab.reshape(n_var, N_KV, n_k, 4 * n_q)


def _band_bias(bias_heads, window, n_prev):
    w = (n_prev + 1) * QB
    v = np.arange(n_prev + 1)[:, None, None]
    rel = v * QB + np.arange(QB)[None, :, None] - np.arange(w)[None, None, :]
    return _bias_tiles(bias_heads, rel, (rel >= 0) & (rel < window))


def _slc_bias(bias_heads):
    d = np.arange(3)[:, None, None]
    rel = d * SLC_QB + np.arange(SLC_QB)[None, :, None] - np.arange(SLC_QB)[None, None, :]
    return _bias_tiles(bias_heads, rel, rel >= 0)


def _stack_heads(q_ref, rows, g, lo):
    parts = []
    for c in range(2):
        col = (2 * g + c) * LANES
        q2 = q_ref[rows, col:col + LANES] * SCALE
        parts += [jnp.where(lo, q2, 0), jnp.where(lo, 0, q2)]
    return jnp.concatenate(parts, axis=0)


def _unstack_heads(o, o_ref, rows, g, lo, n):
    for c in range(2):
        col = (2 * g + c) * LANES
        pair = jnp.where(lo, o[2 * c * n:(2 * c + 1) * n], o[(2 * c + 1) * n:(2 * c + 2) * n])
        o_ref[rows, col:col + LANES] = pair.astype(BF16)


def _banded_kernel(*refs, n_prev, n_sub, has_sinks):
    if has_sinks:
        q_ref, k_ref, v_ref, bias_ref, sink_ref, o_ref = refs
    else:
        q_ref, k_ref, v_ref, bias_ref, o_ref = refs
    w = (n_prev + 1) * QB
    lo = lax.broadcasted_iota(jnp.int32, (1, LANES), 1) < HEAD_DIM
    for u in range(n_sub):
        i = (pl.program_id(0) * n_sub + u) % N_QB
        start = pl.multiple_of(jnp.maximum(i - n_prev, 0) * QB, QB)
        var = jnp.minimum(i, n_prev)
        rows = slice(u * QB, (u + 1) * QB)
        for g in range(N_KV):
            kc = k_ref[pl.ds(start, w), g * LANES:(g + 1) * LANES]
            vc = v_ref[pl.ds(start, w), g * LANES:(g + 1) * LANES]
            st = _dot_nt(kc, _stack_heads(q_ref, rows, g, lo)) + bias_ref[var, g]
            m = jnp.max(st, axis=0, keepdims=True)
            if has_sinks:
                sk = sink_ref[g]
                m = jnp.maximum(m, sk)
            e = jnp.exp(st - m)
            den = jnp.sum(e, axis=0, keepdims=True)
            if has_sinks:
                den = den + jnp.exp(sk - m)
            ot = _dot_tn(vc, e.astype(BF16)) / den
            _unstack_heads(ot.T, o_ref, rows, g, lo, QB)


def _banded_attention(q_all, k_all, v_all, bias, sinks, q_col, kv_col, n_prev, n_sub):
    n_tok = q_all.shape[0]
    has_sinks = sinks is not None
    w = (n_prev + 1) * QB
    tq = QB * n_sub
    in_specs = [pl.BlockSpec((tq, 512), lambda gi: (gi, q_col)),
                pl.BlockSpec((SEQ, 256), lambda gi: (gi // (SEQ // tq), kv_col)),
                pl.BlockSpec((SEQ, 256), lambda gi: (gi // (SEQ // tq), kv_col)),
                pl.BlockSpec((n_prev + 1, N_KV, w, 4 * QB), lambda gi: (0, 0, 0, 0))]
    args = [q_all, k_all, v_all, bias]
    if has_sinks:
        in_specs.append(pl.BlockSpec((N_KV, 1, 4 * QB), lambda gi: (0, 0, 0)))
        args.append(sinks)
    return pl.pallas_call(
        functools.partial(_banded_kernel, n_prev=n_prev, n_sub=n_sub, has_sinks=has_sinks),
        grid=(n_tok // tq,),
        in_specs=in_specs,
        out_specs=pl.BlockSpec((tq, 512), lambda gi: (gi, 0)),
        out_shape=jax.ShapeDtypeStruct((n_tok, 512), BF16),
        compiler_params=_cparams(("arbitrary",)),
    )(*args)


def _compress_kernel(z_ref, pos_ref, w1_ref, w2_ref, o_ref):
    half = CMP_STRIDE * HEAD_DIM
    for kv in range(2):
        for g in range(N_KV):
            z = z_ref[0, 2 * kv + g]
            za_h, za_l = _split_bf16(z + pos_ref[kv, :, 0:half])
            zb_h, zb_l = _split_bf16(z + pos_ref[kv, :, half:2 * half])
            w1a_h, w1a_l = w1_ref[kv, 0, 0:half], w1_ref[kv, 1, 0:half]
            w1b_h, w1b_l = w1_ref[kv, 0, half:2 * half], w1_ref[kv, 1, half:2 * half]
            ha = _dot(za_h, w1a_h) + _dot(za_l, w1a_h) + _dot(za_h, w1a_l)
            hb = _dot(zb_h, w1b_h) + _dot(zb_l, w1b_h) + _dot(zb_h, w1b_l)
            h = ha + pltpu.roll(hb, N_CHUNK - 1, axis=0)
            a = jax.nn.gelu(h)
            a_h, a_l = _split_bf16(a)
            out = (_dot(a_h, w2_ref[kv, 0]) + _dot(a_l, w2_ref[kv, 0]) + _dot(a_h, w2_ref[kv, 1]))
            o_ref[0, 2 * kv + g] = out.astype(BF16)


def _compress(c_out, bsz, cmp_pos_k, cmp_w1_k, cmp_w2_k, cmp_pos_v, cmp_w1_v, cmp_w2_v):
    z = c_out.reshape(bsz, N_CHUNK, CMP_STRIDE, 4, HEAD_DIM)
    z = jnp.transpose(z, (0, 3, 1, 2, 4)).reshape(bsz, 4, N_CHUNK, CMP_STRIDE * HEAD_DIM)
    pos = jnp.stack([cmp_pos_k.reshape(1, -1), cmp_pos_v.reshape(1, -1)])

    def split(w):
        hi = w.astype(BF16)
        return jnp.stack([hi, (w - hi.astype(F32)).astype(BF16)])

    w1 = jnp.stack([split(cmp_w1_k), split(cmp_w1_v)])
    w2 = jnp.stack([split(jnp.concatenate([cmp_w2_k, cmp_w2_k], axis=1)),
                    split(jnp.concatenate([cmp_w2_v, cmp_w2_v], axis=1))])
    return pl.pallas_call(
        _compress_kernel,
        grid=(bsz,),
        in_specs=[pl.BlockSpec((1, 4, N_CHUNK, 1024), lambda b: (b, 0, 0, 0)),
                  pl.BlockSpec((2, 1, 2048), lambda b: (0, 0, 0)),
                  pl.BlockSpec((2, 2, 2048, CMP_HIDDEN), lambda b: (0, 0, 0, 0)),
                  pl.BlockSpec((2, 2, CMP_HIDDEN, LANES), lambda b: (0, 0, 0, 0))],
        out_specs=pl.BlockSpec((1, 4, N_CHUNK, LANES), lambda b: (b, 0, 0, 0)),
        out_shape=jax.ShapeDtypeStruct((bsz, 4, N_CHUNK, LANES), BF16),
        compiler_params=_cparams(("arbitrary",)),
    )(z, pos, w1, w2)


CMP_QB = 256


def _cmp_select_kernel(q_ref, kv_ref, ov_ref, o_ref, sel_ref):
    i = pl.program_id(0) % (SEQ // CMP_QB)
    lo = lax.broadcasted_iota(jnp.int32, (1, LANES), 1) < HEAD_DIM
    blk = lax.broadcasted_iota(jnp.int32, (LANES, 1), 0)
    t = i * CMP_QB + lax.broadcasted_iota(jnp.int32, (1, CMP_QB), 1)
    valid = (blk * CMP_STRIDE + (CMP_LEN - 1)) <= t
    valid4 = jnp.concatenate([valid] * 4, axis=1)
    any_valid = jnp.concatenate([t >= CMP_LEN - 1] * 4, axis=1)
    cur = lax.shift_right_logical(t, int(math.log2(SLC_LEN)))
    forced = (blk == 0) | ((blk <= cur) & (blk > cur - SLC_LOCAL))
    future = blk > cur
    blk_f = blk.astype(F32)
    rows = slice(0, CMP_QB)
    for g in range(N_KV):
        kc = kv_ref[0, g]
        vc = kv_ref[0, 2 + g]
        st = jnp.where(valid4, _dot_nt(kc, _stack_heads(q_ref, rows, g, lo)), NEG_INF)
        m = jnp.max(st, axis=0, keepdims=True)
        e = jnp.exp(st - m)
        p = e / jnp.sum(e, axis=0, keepdims=True)
        p = jnp.where(any_valid, p, 0.0)
        ot = _dot_tn(vc, p.astype(BF16))
        _unstack_heads(ot.T, o_ref, rows, g, lo, CMP_QB)
        psum = (p[:, 0:CMP_QB] + p[:, CMP_QB:2 * CMP_QB]
                + p[:, 2 * CMP_QB:3 * CMP_QB] + p[:, 3 * CMP_QB:4 * CMP_QB])
        p_h, p_l = _split_bf16(psum)
        imp = _dot(ov_ref[...], p_h) + _dot(ov_ref[...], p_l)
        imp = jnp.where(forced, FORCED_SCORE, imp)
        imp = jnp.where(future, NEG_INF, imp)
        imp = jnp.where(blk < N_SLC, imp, -3e38)
        sel = jnp.zeros((LANES, CMP_QB), F32)
        for _ in range(SLC_TOP):
            mx = jnp.max(imp, axis=0, keepdims=True)
            idx = jnp.min(jnp.where(imp == mx, blk_f, 1e9), axis=0, keepdims=True)
            hit = blk_f == idx
            sel = jnp.where(hit, 1.0, sel)
            imp = jnp.where(hit, -3e38, imp)
        sel_ref[g] = sel.astype(BF16)


def _cmp_overlap_matrix():
    nc = (SEQ - CMP_LEN) // CMP_STRIDE + 1
    cs = np.arange(nc)[None, :] * CMP_STRIDE
    ss = np.arange(N_SLC)[:, None] * SLC_LEN
    ov = np.clip(np.minimum(cs + CMP_LEN, ss + SLC_LEN) - np.maximum(cs, ss), 0, None)
    out = np.zeros((LANES, LANES), np.float32)
    out[:N_SLC, :nc] = ov / CMP_LEN
    return jnp.asarray(out, BF16)


def _cmp_select(q_all, kv_cmp):
    n_tok = q_all.shape[0]
    nqb = SEQ // CMP_QB
    return pl.pallas_call(
        _cmp_select_kernel,
        grid=(n_tok // CMP_QB,),
        in_specs=[pl.BlockSpec((CMP_QB, 512), lambda gi: (gi, 1)),
                  pl.BlockSpec((1, 4, N_CHUNK, LANES), lambda gi: (gi // nqb, 0, 0, 0)),
                  pl.BlockSpec((LANES, LANES), lambda gi: (0, 0))],
        out_specs=[pl.BlockSpec((CMP_QB, 512), lambda gi: (gi, 0)),
                   pl.BlockSpec((N_KV, LANES, CMP_QB), lambda gi: (0, 0, gi))],
        out_shape=[jax.ShapeDtypeStruct((n_tok, 512), BF16),
                   jax.ShapeDtypeStruct((N_KV, LANES, n_tok), BF16)],
        compiler_params=_cparams(("arbitrary",)),
    )(q_all, kv_cmp, _cmp_overlap_matrix())


def _selected_kernel(q_ref, k_ref, v_ref, sel_ref, exp_ref, bias_ref, o_ref):
    i = pl.program_id(0) % N_SQB
    lo = lax.broadcasted_iota(jnp.int32, (1, LANES), 1) < HEAD_DIM
    rows = slice(0, SLC_QB)
    qs = [_stack_heads(q_ref, rows, g, lo) for g in range(N_KV)]

    def tile(j, carry):
        ks = pl.multiple_of(j * SLC_QB, SLC_QB)
        d = jnp.minimum(i - j, 2)
        out = []
        for g in range(N_KV):
            m, l, acc = carry[g]
            kt = k_ref[pl.ds(ks, SLC_QB), g * LANES:(g + 1) * LANES]
            vt = v_ref[pl.ds(ks, SLC_QB), g * LANES:(g + 1) * LANES]
            member = _dot(exp_ref[j], sel_ref[g])
            mask_add = (member - 1.0) * (-NEG_INF)
            st = (_dot_nt(kt, qs[g]) + bias_ref[d, g]) + jnp.concatenate([mask_add] * 4, axis=1)
            m_new = jnp.maximum(m, jnp.max(st, axis=0, keepdims=True))
            alpha = jnp.exp(m - m_new)
            e = jnp.exp(st - m_new)
            l = alpha * l + jnp.sum(e, axis=0, keepdims=True)
            acc = acc * alpha + _dot_tn(vt, e.astype(BF16))
            out.append((m_new, l, acc))
        return tuple(out)

    init = tuple((jnp.full((1, 4 * SLC_QB), -3e38, F32), jnp.zeros((1, 4 * SLC_QB), F32),
                  jnp.zeros((LANES, 4 * SLC_QB), F32)) for _ in range(N_KV))
    res = lax.fori_loop(0, i + 1, tile, init)
    for g in range(N_KV):
        _, l, acc = res[g]
        _unstack_heads((acc / l).T, o_ref, rows, g, lo, SLC_QB)


def _slc_expand_matrix():
    key_blk = (np.arange(N_SQB)[:, None, None] * SLC_QB + np.arange(SLC_QB)[None, :, None]) // SLC_LEN
    e = (np.arange(LANES)[None, None, :] == key_blk).astype(np.float32)
    return jnp.asarray(e, BF16)


def _selected_attention(q_all, k_all, v_all, sel, bias):
    n_tok = q_all.shape[0]
    return pl.pallas_call(
        _selected_kernel,
        grid=(n_tok // SLC_QB,),
        in_specs=[pl.BlockSpec((SLC_QB, 512), lambda gi: (gi, 1)),
                  pl.BlockSpec((SEQ, 256), lambda gi: (gi // N_SQB, 1)),
                  pl.BlockSpec((SEQ, 256), lambda gi: (gi // N_SQB, 1)),
                  pl.BlockSpec((N_KV, LANES, SLC_QB), lambda gi: (0, 0, gi)),
                  pl.BlockSpec((N_SQB, SLC_QB, LANES), lambda gi: (0, 0, 0)),
                  pl.BlockSpec((3, N_KV, SLC_QB, 4 * SLC_QB), lambda gi: (0, 0, 0, 0))],
        out_specs=pl.BlockSpec((SLC_QB, 512), lambda gi: (gi, 0)),
        out_shape=jax.ShapeDtypeStruct((n_tok, 512), BF16),
        compiler_params=_cparams(("arbitrary",)),
    )(q_all, k_all, v_all, sel, _slc_expand_matrix(), bias)


MIX_TM = 512


def _layer_norm(x, g, b):
    mu = jnp.mean(x, axis=-1, keepdims=True)
    xc = x - mu
    var = jnp.mean(xc * xc, axis=-1, keepdims=True)
    return xc * lax.rsqrt(var + LN_EPS) * g + b


def _pack_bf16_pair(a, b):
    ia = lax.bitcast_convert_type(a.astype(BF16).astype(F32), jnp.int32)
    ib = lax.bitcast_convert_type(b.astype(BF16).astype(F32), jnp.int32)
    return lax.shift_right_logical(ia, 16) | (ib & -65536)


def _unpack_bf16_pair(w):
    a = lax.bitcast_convert_type(lax.shift_left(w, 16), F32)
    b = lax.bitcast_convert_type(w & -65536, F32)
    return a, b


def _unpack_row(w_ref, k=None):
    parts = []
    for half in range(2):
        parts += list(_unpack_bf16_pair(w_ref[half] if k is None else w_ref[half, k]))
    return jnp.concatenate(parts, axis=1)


def _mix_kernel(x_ref, ya_ref, oc_ref, os_ref, ow_ref, g_ref, m_ref, ge_ref, wa_ref, wb_ref,
                wo_ref, lng_ref, lnb_ref, wr_ref, br_ref, tri_ref,
                h_ref, hw_ref, e_ref, r_ref, g_out_ref, cnt_ref):
    gb = g_ref[...].astype(BF16)
    yb = (_dot(gb, ge_ref[0]) * oc_ref[...].astype(F32)
          + _dot(gb, ge_ref[1]) * os_ref[...].astype(F32)
          + _dot(gb, ge_ref[2]) * ow_ref[...].astype(F32))
    ma = _dot(ya_ref[...], wa_ref[...])
    mb = _dot(yb.astype(BF16), wb_ref[...])
    merged = m_ref[:, 0:D_MODEL].astype(F32) * ma + m_ref[:, D_MODEL:2 * D_MODEL].astype(F32) * mb
    mix = _dot(merged.astype(BF16), wo_ref[...])
    h = _layer_norm(DEEPNORM_ALPHA * x_ref[...] + mix, lng_ref[...], lnb_ref[...])
    h_ref[...] = h
    hw_ref[0] = _pack_bf16_pair(h[:, 0:256], h[:, 256:512])
    hw_ref[1] = _pack_bf16_pair(h[:, 512:768], h[:, 768:1024])
    h_hi, h_lo = _split_bf16(h)
    logits = (_dot_nt(wr_ref[0], h_hi) + _dot_nt(wr_ref[0], h_lo) + _dot_nt(wr_ref[1], h_hi)
              + br_ref[...])
    row = lax.broadcasted_iota(jnp.int32, logits.shape, 0).astype(F32)
    vals, hits, idxs = [], [], []
    v = logits
    for _ in range(TOP_K):
        mx = jnp.max(v, axis=0, keepdims=True)
        idx = jnp.min(jnp.where(v == mx, row, 1e9), axis=0, keepdims=True)
        hit = row == idx
        vals.append(mx)
        hits.append(hit)
        idxs.append(idx)
        v = jnp.where(hit, -3e38, v)
    es = [jnp.exp(t - vals[0]) for t in vals]
    den = es[0] + es[1] + es[2] + es[3]

    @pl.when(pl.program_id(0) == 0)
    def _():
        cnt_ref[...] = jnp.zeros(cnt_ref.shape, F32)

    routed = sum(jnp.where(hit, 1.0, 0.0) for hit in hits)
    before = _dot(routed.astype(BF16), tri_ref[...]) + cnt_ref[:, 0:1]
    for k in range(TOP_K):
        e_ref[k:k + 1, :] = idxs[k].astype(jnp.int32)
        r_ref[k:k + 1, :] = jnp.sum(jnp.where(hits[k], before, 0.0), axis=0,
                                    keepdims=True).astype(jnp.int32)
        g_out_ref[k:k + 1, :] = es[k] / den
    cnt_ref[...] = cnt_ref[...] + jnp.sum(routed, axis=1, keepdims=True)


def _gate_expand_matrix():
    e = np.zeros((3, LANES, 512), np.float32)
    for c in range(3):
        for h in range(N_HEADS):
            e[c, 3 * h + c, h * HEAD_DIM:(h + 1) * HEAD_DIM] = 1.0
    return jnp.asarray(e, BF16)


def _mix(x2d, ya, oc, osl, ow, gates, merge, w_branch_a, w_branch_b, w_out, ln_g, ln_b,
         w_router, b_router):
    n_tok = x2d.shape[0]
    tm = MIX_TM
    row = lambda i: (i, 0)
    fix2 = lambda i: (0, 0)
    fix3 = lambda i: (0, 0, 0)
    wr_t = w_router.T
    wr_hi = wr_t.astype(BF16)
    wr = jnp.stack([wr_hi, (wr_t - wr_hi.astype(F32)).astype(BF16)])
    tri = jnp.asarray(np.triu(np.ones((tm, tm), np.float32), k=1), BF16)
    tok = lambda i: (0, i)
    return pl.pallas_call(
        _mix_kernel,
        grid=(n_tok // tm,),
        in_specs=[pl.BlockSpec((tm, D_MODEL), row),
                  pl.BlockSpec((tm, 512), row), pl.BlockSpec((tm, 512), row),
                  pl.BlockSpec((tm, 512), row), pl.BlockSpec((tm, 512), row),
                  pl.BlockSpec((tm, LANES), row), pl.BlockSpec((tm, 2 * D_MODEL), row),
                  pl.BlockSpec((3, LANES, 512), fix3),
                  pl.BlockSpec((512, D_MODEL), fix2), pl.BlockSpec((512, D_MODEL), fix2),
                  pl.BlockSpec((D_MODEL, D_MODEL), fix2),
                  pl.BlockSpec((1, D_MODEL), fix2), pl.BlockSpec((1, D_MODEL), fix2),
                  pl.BlockSpec((2, N_EXPERTS, D_MODEL), fix3),
                  pl.BlockSpec((N_EXPERTS, 1), fix2),
                  pl.BlockSpec((tm, tm), fix2)],
        out_specs=[pl.BlockSpec((tm, D_MODEL), row),
                   pl.BlockSpec((2, tm, 256), lambda i: (0, i, 0)),
                   pl.BlockSpec((TOP_K, tm), tok), pl.BlockSpec((TOP_K, tm), tok),
                   pl.BlockSpec((TOP_K, tm), tok),
                   pl.BlockSpec((N_EXPERTS, LANES), fix2)],
        out_shape=[jax.ShapeDtypeStruct((n_tok, D_MODEL), F32),
                   jax.ShapeDtypeStruct((2, n_tok, 256), jnp.int32),
                   jax.ShapeDtypeStruct((TOP_K, n_tok), jnp.int32),
                   jax.ShapeDtypeStruct((TOP_K, n_tok), jnp.int32),
                   jax.ShapeDtypeStruct((TOP_K, n_tok), F32),
                   jax.ShapeDtypeStruct((N_EXPERTS, LANES), F32)],
        compiler_params=_cparams(("arbitrary",)),
    )(x2d, ya, oc, osl, ow, gates, merge, _gate_expand_matrix(),
      w_branch_a.astype(BF16), w_branch_b.astype(BF16), w_out.astype(BF16),
      ln_g[None, :], ln_b[None, :], wr, b_router[:, None], tri)


def _sc_mesh():
    return plsc.VectorSubcoreMesh(core_axis_name="core", subcore_axis_name="subcore")


def _sc_scatter_rows(x, idx, n_out):
    n_tok = x.shape[1]
    n_win = n_tok // SC_WINDOW
    x2 = x.reshape(2 * n_tok, 256)

    @pl.kernel(out_type=jax.ShapeDtypeStruct((n_out, 256), x.dtype), mesh=_sc_mesh(),
               scratch_types=[])
    def scatter(x_hbm, i_hbm, o_hbm):
        def body(x_vmem, i_vmem):
            pltpu.sync_copy(x_vmem, o_hbm.at[i_vmem.at[0]])

        pltpu.emit_pipeline(
            body, grid=(2 * TOP_K * n_win,),
            in_specs=[pl.BlockSpec((SC_WINDOW, 256),
                                   lambda c: ((c // (TOP_K * n_win)) * n_win + c % n_win, 0)),
                      pl.BlockSpec((1, SC_WINDOW), lambda c: (0, c))],
            out_specs=[], core_axis_name=("core", "subcore"),
            dimension_semantics=(pltpu.PARALLEL,))(x_hbm, i_hbm)

    return scatter(x2, idx)


def _sc_gather_rows(y, idx):
    n_idx = idx.shape[1]

    @pl.kernel(out_type=jax.ShapeDtypeStruct((n_idx, 256), y.dtype), mesh=_sc_mesh())
    def gather(y_hbm, i_hbm, o_hbm):
        def body(i_vmem, o_vmem):
            pltpu.sync_copy(y_hbm.at[i_vmem.at[0]], o_vmem)

        pltpu.emit_pipeline(
            body, grid=(n_idx // SC_WINDOW,),
            in_specs=[pl.BlockSpec((1, SC_WINDOW), lambda c: (0, c))],
            out_specs=[pl.BlockSpec((SC_WINDOW, 256), lambda c: (c, 0))],
            core_axis_name=("core", "subcore"),
            dimension_semantics=(pltpu.PARALLEL,))(i_hbm, o_hbm)

    return gather(y, idx)


def _ffn_kernel(be_ref, nu_ref, r_ref, wgu_ref, bgu_ref, wd_ref, bd_ref, y_ref, wgu_s, wd_s):
    i = pl.program_id(0)

    @pl.when(i < nu_ref[0])
    def _():
        @pl.when((i == 0) | (be_ref[i] != be_ref[jnp.maximum(i - 1, 0)]))
        def _():
            wgu_s[...] = wgu_ref[0].astype(BF16)
            wd_s[...] = wd_ref[0].astype(BF16)

        x = _unpack_row(r_ref).astype(BF16)
        hcat = _dot(x, wgu_s[...]) + bgu_ref[0]
        glu = jnp.minimum(hcat[:, 0:D_FF], SWIGLU_LIMIT)
        lin = jnp.clip(hcat[:, D_FF:2 * D_FF], -SWIGLU_LIMIT, SWIGLU_LIMIT)
        act = glu * jax.nn.sigmoid(SWIGLU_ALPHA * glu) * (lin + 1.0)
        y = _dot(act.astype(BF16), wd_s[...]) + bd_ref[0]
        y_ref[0] = _pack_bf16_pair(y[:, 0:256], y[:, 256:512])
        y_ref[1] = _pack_bf16_pair(y[:, 512:768], y[:, 768:1024])


def _expert_ffn(rows, block_expert, n_used, w_gate_up, b_gate_up, w_down, b_down):
    n_slots = rows.shape[1]
    grid_spec = pltpu.PrefetchScalarGridSpec(
        num_scalar_prefetch=2,
        grid=(n_slots // MOE_BLOCK,),
        in_specs=[pl.BlockSpec((2, MOE_BLOCK, 256), lambda i, be, nu: (0, i, 0)),
                  pl.BlockSpec((1, D_MODEL, 2 * D_FF), lambda i, be, nu: (be[i], 0, 0)),
                  pl.BlockSpec((1, 1, 2 * D_FF), lambda i, be, nu: (be[i], 0, 0)),
                  pl.BlockSpec((1, D_FF, D_MODEL), lambda i, be, nu: (be[i], 0, 0)),
                  pl.BlockSpec((1, 1, D_MODEL), lambda i, be, nu: (be[i], 0, 0))],
        out_specs=pl.BlockSpec((2, MOE_BLOCK, 256), lambda i, be, nu: (0, i, 0)),
        scratch_shapes=[pltpu.VMEM((D_MODEL, 2 * D_FF), BF16), pltpu.VMEM((D_FF, D_MODEL), BF16)])
    return pl.pallas_call(
        _ffn_kernel,
        grid_spec=grid_spec,
        out_shape=jax.ShapeDtypeStruct((2, n_slots, 256), jnp.int32),
        compiler_params=_cparams(("arbitrary",)),
    )(block_expert, n_used, rows, w_gate_up, b_gate_up[:, None, :], w_down, b_down[:, None, :])


def _dispatch_plan(eidx, rank, counts, n_tok):
    n_slots = n_tok * TOP_K + N_EXPERTS * MOE_BLOCK
    cnt = counts[:, 0].astype(jnp.int32)
    padded = (cnt + (MOE_BLOCK - 1)) // MOE_BLOCK * MOE_BLOCK
    ends = jnp.cumsum(padded)
    base = ends - padded
    onehot = eidx[..., None] == jnp.arange(N_EXPERTS, dtype=jnp.int32)
    slot = jnp.sum(jnp.where(onehot, base, 0), axis=-1) + rank
    blk_start = jnp.arange(n_slots // MOE_BLOCK, dtype=jnp.int32) * MOE_BLOCK
    block_expert = jnp.sum((ends[None, :] <= blk_start[:, None]).astype(jnp.int32), axis=1)
    block_expert = jnp.minimum(block_expert, N_EXPERTS - 1)
    n_used = (ends[-1:] // MOE_BLOCK).astype(jnp.int32)
    idx = jnp.concatenate([slot, slot + n_slots], axis=0).reshape(1, 2 * TOP_K * n_tok)
    return n_slots, idx, block_expert, n_used


def _moe(hw, eidx, rank, counts, w_gate_up, b_gate_up, w_down, b_down):
    n_tok = hw.shape[1]
    n_slots, idx, block_expert, n_used = _dispatch_plan(eidx, rank, counts, n_tok)
    rows = _sc_scatter_rows(hw, idx, 2 * n_slots).reshape(2, n_slots, 256)
    y = _expert_ffn(rows, block_expert, n_used, w_gate_up, b_gate_up, w_down, b_down)
    yg = _sc_gather_rows(y.reshape(2 * n_slots, 256), idx)
    return yg.reshape(2, TOP_K, n_tok, 256)


def _combine_ln_kernel(h_ref, yg_ref, w_ref, g_ref, b_ref, o_ref):
    ffn = jnp.zeros(h_ref.shape, F32)
    for k in range(TOP_K):
        ffn = ffn + w_ref[:, k:k + 1] * _unpack_row(yg_ref, k)
    o_ref[...] = _layer_norm(DEEPNORM_ALPHA * h_ref[...] + ffn, g_ref[...], b_ref[...])


def _combine_ln(h, yg, w_tok, g, b):
    n_tok = h.shape[0]
    tm = 512
    row = lambda i: (i, 0)
    fix = lambda i: (0, 0)
    return pl.pallas_call(
        _combine_ln_kernel,
        grid=(n_tok // tm,),
        in_specs=[pl.BlockSpec((tm, D_MODEL), row),
                  pl.BlockSpec((2, TOP_K, tm, 256), lambda i: (0, 0, i, 0)),
                  pl.BlockSpec((tm, TOP_K), row),
                  pl.BlockSpec((1, D_MODEL), fix), pl.BlockSpec((1, D_MODEL), fix)],
        out_specs=pl.BlockSpec((tm, D_MODEL), row),
        out_shape=jax.ShapeDtypeStruct((n_tok, D_MODEL), F32),
        compiler_params=_cparams(("arbitrary",)),
    )(h, yg, w_tok, g[None, :], b[None, :])


def _layer(x2d, bsz, w_in, b_in, rel_bias, attn_sinks, cmp_pos_k, cmp_w1_k, cmp_w2_k, cmp_pos_v,
           cmp_w1_v, cmp_w2_v, w_branch_a, w_branch_b, w_out, ln1_g, ln1_b, w_router, b_router,
           w_gate_up, b_gate_up, w_down, b_down, ln2_g, ln2_b):
    w_packed, b_packed = _pack_in_weights(w_in, b_in)
    q_all, k_all, v_all, gates, merge, c_out = _in_projection(x2d, w_packed, b_packed)

    bias_a = rel_bias[:, :N_HEADS]
    bias_b = rel_bias[:, N_HEADS:]
    sinks = jnp.broadcast_to(attn_sinks.reshape(N_KV, 1, 4, 1), (N_KV, 1, 4, QB))
    sinks = sinks.reshape(N_KV, 1, 4 * QB)
    ya = _banded_attention(q_all, k_all, v_all, _band_bias(bias_a, A_WINDOW, 1), sinks,
                           q_col=0, kv_col=0, n_prev=1, n_sub=4)
    ow = _banded_attention(q_all, k_all, v_all, _band_bias(bias_b, B_WINDOW, 4), None,
                           q_col=1, kv_col=2, n_prev=4, n_sub=2)
    kv_cmp = _compress(c_out, bsz, cmp_pos_k, cmp_w1_k, cmp_w2_k, cmp_pos_v, cmp_w1_v, cmp_w2_v)
    oc, sel = _cmp_select(q_all, kv_cmp)
    osl = _selected_attention(q_all, k_all, v_all, sel, _slc_bias(bias_b))

    h, hw, eidx, rank, gate_w, counts = _mix(x2d, ya, oc, osl, ow, gates, merge, w_branch_a,
                                             w_branch_b, w_out, ln1_g, ln1_b, w_router, b_router)
    yg = _moe(hw, eidx, rank, counts, w_gate_up, b_gate_up, w_down, b_down)
    return _combine_ln(h, yg, gate_w.T, ln2_g, ln2_b)


def kernel(x, w_in, b_in, rel_bias, attn_sinks, cmp_pos_k, cmp_w1_k, cmp_w2_k, cmp_pos_v, cmp_w1_v,
           cmp_w2_v, w_branch_a, w_branch_b, w_out, ln1_g, ln1_b, w_router, b_router, w_gate_up,
           b_gate_up, w_down, b_down, ln2_g, ln2_b):
    bsz, seq, d = x.shape
    assert seq == SEQ and d == D_MODEL
    h = x.reshape(bsz * seq, d)
    for l in range(w_in.shape[0]):
        h = _layer(h, bsz, w_in[l], b_in[l], rel_bias, attn_sinks[l], cmp_pos_k[l], cmp_w1_k[l],
                   cmp_w2_k[l], cmp_pos_v[l], cmp_w1_v[l], cmp_w2_v[l], w_branch_a[l],
                   w_branch_b[l], w_out[l], ln1_g[l], ln1_b[l], w_router[l], b_router[l],
                   w_gate_up[l], b_gate_up[l], w_down[l], b_down[l], ln2_g[l], ln2_b[l])
    return h.reshape(bsz, seq, d)
```

```python
import functools
import math

import jax
import jax.numpy as jnp
import numpy as np
from jax import lax
from jax.experimental import pallas as pl
from jax.experimental.pallas import tpu as pltpu
from jax.experimental.pallas import tpu_sc as plsc

F32 = jnp.float32
BF16 = jnp.bfloat16

D_MODEL = 1024
SEQ = 2048
HEAD_DIM = 64
N_HEADS = 8
N_KV = 2
A_WINDOW = 128
B_WINDOW = 512
CMP_LEN = 32
CMP_STRIDE = 16
CMP_HIDDEN = 128
SLC_LEN = 64
SLC_TOP = 8
SLC_LOCAL = 2
N_BUCKETS = 32
REL_MAX_DIST = 128
N_EXPERTS = 32
TOP_K = 4
D_FF = D_MODEL
SWIGLU_LIMIT = 7.0
SWIGLU_ALPHA = 1.702
LN_EPS = 1e-5
NEG_INF = -1e30
FORCED_SCORE = 1e30
DEEPNORM_ALPHA = 2.0 ** 0.25
LOG2E = math.log2(math.e)
Q_SCALE = HEAD_DIM ** -0.5 * LOG2E

LANES = 128
QB = 128
N_QB = SEQ // QB
SLC_QB = 256
N_SQB = SEQ // SLC_QB
N_CHUNK = SEQ // CMP_STRIDE
N_SLC = SEQ // SLC_LEN
VMEM_LIMIT = 56 * 1024 * 1024

IN_OFF = dict(qa=0, ka=512, va=640, qb=768, kbc=1280, vbc=1408, kbs=1536, vbs=1664,
              kbw=1792, vbw=1920, gate=2048, merge=2072)
PROJ_TM = 512
MOE_BLOCK = 512
SC_WINDOW = 128


def _cparams(sem):
    return pltpu.CompilerParams(dimension_semantics=sem, vmem_limit_bytes=VMEM_LIMIT)


def _dot(a, b):
    return jnp.dot(a, b, preferred_element_type=F32)


def _dot_nt(a, b):
    return lax.dot_general(a, b, (((1,), (1,)), ((), ())), preferred_element_type=F32)


def _dot_tn(a, b):
    return lax.dot_general(a, b, (((0,), (0,)), ((), ())), preferred_element_type=F32)


def _split_bf16(x):
    hi = x.astype(BF16)
    lo = (x - hi.astype(F32)).astype(BF16)
    return hi, lo


def _proj_kernel(x_ref, w_ref, b_ref, q_ref, k_ref, v_ref, g_ref, m_ref, c_ref):
    xb = x_ref[...].astype(BF16)

    def mm(c0, c1):
        return _dot(xb, w_ref[:, c0:c1]) + b_ref[:, c0:c1]

    for c in range(0, 1024, 512):
        q_ref[:, c:c + 512] = mm(c, c + 512).astype(BF16)
    k_ref[...] = mm(1024, 1792).astype(BF16)
    v_ref[...] = mm(1792, 2560).astype(BF16)
    g_ref[...] = jax.nn.sigmoid(mm(2560, 2688))
    for c in range(0, 2048, 512):
        m_ref[:, c:c + 512] = jax.nn.sigmoid(mm(2688 + c, 2688 + c + 512)).astype(BF16)
    c_ref[...] = mm(4736, 4992)


def _pack_in_weights(w_in, b_in):
    def cols(name, width):
        o = IN_OFF[name]
        return w_in[:, o:o + width], b_in[o:o + width]

    def dup_groups(name):
        w, b = cols(name, 128)
        ws, bs = [], []
        for g in range(N_KV):
            wg, bg = w[:, g * 64:(g + 1) * 64], b[g * 64:(g + 1) * 64]
            ws += [wg, wg]
            bs += [bg, bg]
        return jnp.concatenate(ws, axis=1), jnp.concatenate(bs)

    def scaled(name, width):
        w, b = cols(name, width)
        return w * Q_SCALE, b * Q_SCALE

    parts = [scaled('qa', 512), scaled('qb', 512),
             dup_groups('ka'), dup_groups('kbs'), dup_groups('kbw'),
             dup_groups('va'), dup_groups('vbs'), dup_groups('vbw')]
    wg, bg = cols('gate', 24)
    parts.append((jnp.pad(wg, ((0, 0), (0, 104))), jnp.pad(bg, (0, 104))))
    parts.append(cols('merge', 2048))
    parts.append(cols('kbc', 128))
    parts.append(cols('vbc', 128))
    w = jnp.concatenate([p[0] for p in parts], axis=1).astype(BF16)
    b = jnp.concatenate([p[1] for p in parts])[None, :]
    return w, b


def _in_projection(x2d, w_packed, b_packed):
    n_tok = x2d.shape[0]
    n_col = w_packed.shape[1]
    tm = PROJ_TM
    row = lambda i: (i, 0)
    fixed = lambda i: (0, 0)
    widths = (1024, 768, 768, 128, 2048, 256)
    dtypes = (BF16, BF16, BF16, F32, BF16, F32)
    return pl.pallas_call(
        _proj_kernel,
        grid=(n_tok // tm,),
        in_specs=[pl.BlockSpec((tm, D_MODEL), row),
                  pl.BlockSpec((D_MODEL, n_col), fixed),
                  pl.BlockSpec((1, n_col), fixed)],
        out_specs=[pl.BlockSpec((tm, w), row) for w in widths],
        out_shape=[jax.ShapeDtypeStruct((n_tok, w), dt) for w, dt in zip(widths, dtypes)],
        compiler_params=_cparams(("arbitrary",)),
    )(x2d, w_packed, b_packed)


def _bucket_np(rel):
    n = np.maximum(rel, 0)
    max_exact = N_BUCKETS // 2
    nf = np.maximum(n, 1).astype(np.float32)
    large = max_exact + (np.log(nf / max_exact) / math.log(REL_MAX_DIST / max_exact)
                         * (N_BUCKETS - max_exact)).astype(np.int32)
    large = np.minimum(large, N_BUCKETS - 1)
    return np.where(n < max_exact, n, large)


def _bias_tiles(bias_heads, rel, valid):
    n_var, n_q, n_k = rel.shape
    onehot = (jnp.asarray(_bucket_np(rel), jnp.int32)[..., None]
              == jnp.arange(N_BUCKETS, dtype=jnp.int32)).astype(F32)
    tab = jnp.einsum('vack,kh->vhca', onehot, bias_heads, precision=lax.Precision.HIGHEST)
    tab = jnp.where(jnp.asarray(np.swapaxes(valid, 1, 2))[:, None], tab * LOG2E, NEG_INF)
    tab = tab.reshape(n_var, N_KV, 4, n_k, n_q).transpose(0, 1, 3, 2, 4)
    return tab.reshape(n_var, N_KV, n_k, 4 * n_q)


def _band_bias(bias_heads, window, n_prev):
    w = (n_prev + 1) * QB
    v = np.arange(n_prev + 1)[:, None, None]
    rel = v * QB + np.arange(QB)[None, :, None] - np.arange(w)[None, None, :]
    return _bias_tiles(bias_heads, rel, (rel >= 0) & (rel < window))


def _slc_bias(bias_heads):
    d = np.arange(2)[:, None, None]
    rel = d * SLC_QB + np.arange(SLC_QB)[None, :, None] - np.arange(SLC_QB)[None, None, :]
    return _bias_tiles(bias_heads, rel, rel >= 0)


def _stack_heads(q_ref, rows, g, lo):
    parts = []
    for c in range(2):
        col = (2 * g + c) * LANES
        q2 = q_ref[rows, col:col + LANES]
        parts += [jnp.where(lo, q2, 0), jnp.where(lo, 0, q2)]
    return jnp.concatenate(parts, axis=0)


def _unstack_heads(o, o_ref, rows, g, lo, n):
    for c in range(2):
        col = (2 * g + c) * LANES
        odd = pltpu.roll(o[(2 * c + 1) * n:(2 * c + 2) * n], HEAD_DIM, axis=1)
        pair = jnp.where(lo, o[2 * c * n:(2 * c + 1) * n], odd)
        o_ref[rows, col:col + LANES] = pair.astype(BF16)


def _with_ones_lane(v, lane):
    return jnp.where(lane < HEAD_DIM, v, jnp.where(lane == HEAD_DIM, 1.0, 0.0).astype(v.dtype))


def _banded_kernel(*refs, n_prev, n_sub, has_sinks):
    if has_sinks:
        q_ref, k_ref, v_ref, bias_ref, sink_ref, o_ref = refs
    else:
        q_ref, k_ref, v_ref, bias_ref, o_ref = refs
    w = (n_prev + 1) * QB
    lane = lax.broadcasted_iota(jnp.int32, (1, LANES), 1)
    lo = lane < HEAD_DIM
    for u in range(n_sub):
        i = (pl.program_id(0) * n_sub + u) % N_QB
        start = pl.multiple_of(jnp.maximum(i - n_prev, 0) * QB, QB)
        var = jnp.minimum(i, n_prev)
        rows = slice(u * QB, (u + 1) * QB)
        for g in range(N_KV):
            kc = k_ref[pl.ds(start, w), g * LANES:(g + 1) * LANES]
            vc = _with_ones_lane(v_ref[pl.ds(start, w), g * LANES:(g + 1) * LANES], lane)
            st = _dot_nt(kc, _stack_heads(q_ref, rows, g, lo)) + bias_ref[var, g]
            m = jnp.max(st, axis=0, keepdims=True)
            if has_sinks:
                sk = sink_ref[g]
                m = jnp.maximum(m, sk)
            ot = _dot_tn(vc, jnp.exp2(st - m).astype(BF16))
            den = ot[HEAD_DIM:HEAD_DIM + 1, :]
            if has_sinks:
                den = den + jnp.exp2(sk - m)
            _unstack_heads((ot / den).T, o_ref, rows, g, lo, QB)


def _banded_attention(q_all, k_all, v_all, bias, sinks, q_col, kv_col, n_prev, n_sub):
    n_tok = q_all.shape[0]
    has_sinks = sinks is not None
    w = (n_prev + 1) * QB
    tq = QB * n_sub
    in_specs = [pl.BlockSpec((tq, 512), lambda gi: (gi, q_col)),
                pl.BlockSpec((SEQ, 256), lambda gi: (gi // (SEQ // tq), kv_col)),
                pl.BlockSpec((SEQ, 256), lambda gi: (gi // (SEQ // tq), kv_col)),
                pl.BlockSpec((n_prev + 1, N_KV, w, 4 * QB), lambda gi: (0, 0, 0, 0))]
    args = [q_all, k_all, v_all, bias]
    if has_sinks:
        in_specs.append(pl.BlockSpec((N_KV, 1, 4 * QB), lambda gi: (0, 0, 0)))
        args.append(sinks)
    return pl.pallas_call(
        functools.partial(_banded_kernel, n_prev=n_prev, n_sub=n_sub, has_sinks=has_sinks),
        grid=(n_tok // tq,),
        in_specs=in_specs,
        out_specs=pl.BlockSpec((tq, 512), lambda gi: (gi, 0)),
        out_shape=jax.ShapeDtypeStruct((n_tok, 512), BF16),
        compiler_params=_cparams(("arbitrary",)),
    )(*args)


def _compress_kernel(z_ref, pos_ref, w1_ref, w2_ref, o_ref):
    half = CMP_STRIDE * HEAD_DIM
    for kv in range(2):
        for g in range(N_KV):
            z = z_ref[0, 2 * kv + g]
            za_h, za_l = _split_bf16(z + pos_ref[kv, :, 0:half])
            zb_h, zb_l = _split_bf16(z + pos_ref[kv, :, half:2 * half])
            w1a_h, w1a_l = w1_ref[kv, 0, 0:half], w1_ref[kv, 1, 0:half]
            w1b_h, w1b_l = w1_ref[kv, 0, half:2 * half], w1_ref[kv, 1, half:2 * half]
            ha = _dot(za_h, w1a_h) + _dot(za_l, w1a_h) + _dot(za_h, w1a_l)
            hb = _dot(zb_h, w1b_h) + _dot(zb_l, w1b_h) + _dot(zb_h, w1b_l)
            h = ha + pltpu.roll(hb, N_CHUNK - 1, axis=0)
            a = jax.nn.gelu(h)
            a_h, a_l = _split_bf16(a)
            out = (_dot(a_h, w2_ref[kv, 0]) + _dot(a_l, w2_ref[kv, 0]) + _dot(a_h, w2_ref[kv, 1]))
            o_ref[0, 2 * kv + g] = out.astype(BF16)


def _compress(c_out, bsz, cmp_pos_k, cmp_w1_k, cmp_w2_k, cmp_pos_v, cmp_w1_v, cmp_w2_v):
    z = c_out.reshape(bsz, N_CHUNK, CMP_STRIDE, 4, HEAD_DIM)
    z = jnp.transpose(z, (0, 3, 1, 2, 4)).reshape(bsz, 4, N_CHUNK, CMP_STRIDE * HEAD_DIM)
    pos = jnp.stack([cmp_pos_k.reshape(1, -1), cmp_pos_v.reshape(1, -1)])

    def split(w):
        hi = w.astype(BF16)
        return jnp.stack([hi, (w - hi.astype(F32)).astype(BF16)])

    w1 = jnp.stack([split(cmp_w1_k), split(cmp_w1_v)])
    w2 = jnp.stack([split(jnp.concatenate([cmp_w2_k, cmp_w2_k], axis=1)),
                    split(jnp.concatenate([cmp_w2_v, cmp_w2_v], axis=1))])
    return pl.pallas_call(
        _compress_kernel,
        grid=(bsz,),
        in_specs=[pl.BlockSpec((1, 4, N_CHUNK, 1024), lambda b: (b, 0, 0, 0)),
                  pl.BlockSpec((2, 1, 2048), lambda b: (0, 0, 0)),
                  pl.BlockSpec((2, 2, 2048, CMP_HIDDEN), lambda b: (0, 0, 0, 0)),
                  pl.BlockSpec((2, 2, CMP_HIDDEN, LANES), lambda b: (0, 0, 0, 0))],
        out_specs=pl.BlockSpec((1, 4, N_CHUNK, LANES), lambda b: (b, 0, 0, 0)),
        out_shape=jax.ShapeDtypeStruct((bsz, 4, N_CHUNK, LANES), BF16),
        compiler_params=_cparams(("arbitrary",)),
    )(z, pos, w1, w2)


CMP_QB = 256


def _cmp_select_kernel(q_ref, kv_ref, ov_ref, o_ref, sel_ref):
    i = pl.program_id(0) % (SEQ // CMP_QB)
    lo = lax.broadcasted_iota(jnp.int32, (1, LANES), 1) < HEAD_DIM
    blk = lax.broadcasted_iota(jnp.int32, (LANES, 1), 0)
    t = i * CMP_QB + lax.broadcasted_iota(jnp.int32, (1, CMP_QB), 1)
    valid = (blk * CMP_STRIDE + (CMP_LEN - 1)) <= t
    valid4 = jnp.concatenate([valid] * 4, axis=1)
    any_valid = jnp.concatenate([t >= CMP_LEN - 1] * 4, axis=1)
    cur = lax.shift_right_logical(t, int(math.log2(SLC_LEN)))
    forced = (blk == 0) | ((blk <= cur) & (blk > cur - SLC_LOCAL))
    future = blk > cur
    blk_f = blk.astype(F32)
    rows = slice(0, CMP_QB)
    for g in range(N_KV):
        kc = kv_ref[0, g]
        vc = kv_ref[0, 2 + g]
        st = jnp.where(valid4, _dot_nt(kc, _stack_heads(q_ref, rows, g, lo)), NEG_INF)
        m = jnp.max(st, axis=0, keepdims=True)
        e = jnp.exp2(st - m)
        p = e / jnp.sum(e, axis=0, keepdims=True)
        p = jnp.where(any_valid, p, 0.0)
        ot = _dot_tn(vc, p.astype(BF16))
        _unstack_heads(ot.T, o_ref, rows, g, lo, CMP_QB)
        psum = (p[:, 0:CMP_QB] + p[:, CMP_QB:2 * CMP_QB]
                + p[:, 2 * CMP_QB:3 * CMP_QB] + p[:, 3 * CMP_QB:4 * CMP_QB])
        p_h, p_l = _split_bf16(psum)
        imp = _dot(ov_ref[...], p_h) + _dot(ov_ref[...], p_l)
        imp = jnp.where(forced, FORCED_SCORE, imp)
        imp = jnp.where(future, NEG_INF, imp)
        imp = jnp.where(blk < N_SLC, imp, -3e38)
        sel = jnp.zeros((LANES, CMP_QB), F32)
        for _ in range(SLC_TOP):
            mx = jnp.max(imp, axis=0, keepdims=True)
            idx = jnp.min(jnp.where(imp == mx, blk_f, 1e9), axis=0, keepdims=True)
            hit = blk_f == idx
            sel = jnp.where(hit, 1.0, sel)
            imp = jnp.where(hit, -3e38, imp)
        sel_ref[:, g * LANES:(g + 1) * LANES] = sel.T.astype(BF16)


def _cmp_overlap_matrix():
    nc = (SEQ - CMP_LEN) // CMP_STRIDE + 1
    cs = np.arange(nc)[None, :] * CMP_STRIDE
    ss = np.arange(N_SLC)[:, None] * SLC_LEN
    ov = np.clip(np.minimum(cs + CMP_LEN, ss + SLC_LEN) - np.maximum(cs, ss), 0, None)
    out = np.zeros((LANES, LANES), np.float32)
    out[:N_SLC, :nc] = ov / CMP_LEN
    return jnp.asarray(out, BF16)


def _cmp_select(q_all, kv_cmp):
    n_tok = q_all.shape[0]
    nqb = SEQ // CMP_QB
    return pl.pallas_call(
        _cmp_select_kernel,
        grid=(n_tok // CMP_QB,),
        in_specs=[pl.BlockSpec((CMP_QB, 512), lambda gi: (gi, 1)),
                  pl.BlockSpec((1, 4, N_CHUNK, LANES), lambda gi: (gi // nqb, 0, 0, 0)),
                  pl.BlockSpec((LANES, LANES), lambda gi: (0, 0))],
        out_specs=[pl.BlockSpec((CMP_QB, 512), lambda gi: (gi, 0)),
                   pl.BlockSpec((CMP_QB, N_KV * LANES), lambda gi: (gi, 0))],
        out_shape=[jax.ShapeDtypeStruct((n_tok, 512), BF16),
                   jax.ShapeDtypeStruct((n_tok, N_KV * LANES), BF16)],
        compiler_params=_cparams(("arbitrary",)),
    )(q_all, kv_cmp, _cmp_overlap_matrix())


def _selected_kernel(q_ref, k_ref, v_ref, sel_ref, far_ref, exp_ref, bias_ref, o_ref):
    i = pl.program_id(0) % N_SQB
    lane = lax.broadcasted_iota(jnp.int32, (1, LANES), 1)
    lo = lane < HEAD_DIM
    rows = slice(0, SLC_QB)
    rhs = []
    for g in range(N_KV):
        member = sel_ref[:, g * LANES:(g + 1) * LANES].astype(F32)
        mask = jnp.where(lane < N_SLC, (member - 1.0) * (-NEG_INF), 0.0)
        side = (jnp.concatenate([mask] * 4, axis=0) + far_ref[g]).astype(BF16)
        rhs.append(jnp.concatenate([_stack_heads(q_ref, rows, g, lo), side], axis=1))

    def tile(j, carry):
        ks = pl.multiple_of(j * SLC_QB, SLC_QB)
        d = i - j
        key_side = exp_ref[jnp.minimum(jnp.maximum(d - 1, 0), 1), j]
        out = []
        for g in range(N_KV):
            m, acc = carry[g]
            kt = k_ref[pl.ds(ks, SLC_QB), g * LANES:(g + 1) * LANES]
            vt = _with_ones_lane(v_ref[pl.ds(ks, SLC_QB), g * LANES:(g + 1) * LANES], lane)
            st = _dot_nt(jnp.concatenate([kt, key_side], axis=1), rhs[g])
            st = lax.cond(d < 2, lambda s: s + bias_ref[jnp.minimum(d, 1), g], lambda s: s, st)
            m_new = jnp.maximum(m, jnp.max(st, axis=0, keepdims=True))
            acc = (acc * jnp.exp2(m - m_new)
                   + _dot_tn(vt, jnp.exp2(st - m_new).astype(BF16)))
            out.append((m_new, acc))
        return tuple(out)

    init = tuple((jnp.full((1, 4 * SLC_QB), -3e38, F32), jnp.zeros((LANES, 4 * SLC_QB), F32))
                 for _ in range(N_KV))
    res = lax.fori_loop(0, i + 1, tile, init)
    for g in range(N_KV):
        acc = res[g][1]
        _unstack_heads((acc / acc[HEAD_DIM:HEAD_DIM + 1, :]).T, o_ref, rows, g, lo, SLC_QB)


FAR_LANE = N_SLC


def _slc_key_side():
    key_blk = (np.arange(N_SQB)[:, None, None] * SLC_QB + np.arange(SLC_QB)[None, :, None]) // SLC_LEN
    e = (np.arange(LANES)[None, None, :] == key_blk).astype(np.float32)
    far = e.copy()
    far[:, :, FAR_LANE:FAR_LANE + 2] = 1.0
    return jnp.asarray(np.stack([e, far]), BF16)


def _slc_far_bias(bias_heads):
    c = bias_heads[N_BUCKETS - 1] * LOG2E
    hi = c.astype(BF16).astype(F32)
    lo = (c - hi).astype(BF16).astype(F32)
    tab = jnp.zeros((N_HEADS, LANES), F32).at[:, FAR_LANE].set(hi).at[:, FAR_LANE + 1].set(lo)
    tab = jnp.broadcast_to(tab.reshape(N_KV, 4, 1, LANES), (N_KV, 4, SLC_QB, LANES))
    return tab.reshape(N_KV, 4 * SLC_QB, LANES)


def _selected_attention(q_all, k_all, v_all, sel, bias_heads):
    n_tok = q_all.shape[0]
    return pl.pallas_call(
        _selected_kernel,
        grid=(n_tok // SLC_QB,),
        in_specs=[pl.BlockSpec((SLC_QB, 512), lambda gi: (gi, 1)),
                  pl.BlockSpec((SEQ, 256), lambda gi: (gi // N_SQB, 1)),
                  pl.BlockSpec((SEQ, 256), lambda gi: (gi // N_SQB, 1)),
                  pl.BlockSpec((SLC_QB, N_KV * LANES), lambda gi: (gi, 0)),
                  pl.BlockSpec((N_KV, 4 * SLC_QB, LANES), lambda gi: (0, 0, 0)),
                  pl.BlockSpec((2, N_SQB, SLC_QB, LANES), lambda gi: (0, 0, 0, 0)),
                  pl.BlockSpec((2, N_KV, SLC_QB, 4 * SLC_QB), lambda gi: (0, 0, 0, 0))],
        out_specs=pl.BlockSpec((SLC_QB, 512), lambda gi: (gi, 0)),
        out_shape=jax.ShapeDtypeStruct((n_tok, 512), BF16),
        compiler_params=_cparams(("arbitrary",)),
    )(q_all, k_all, v_all, sel, _slc_far_bias(bias_heads), _slc_key_side(),
      _slc_bias(bias_heads))


MIX_TM = 512


def _layer_norm(x, g, b):
    mu = jnp.mean(x, axis=-1, keepdims=True)
    xc = x - mu
    var = jnp.mean(xc * xc, axis=-1, keepdims=True)
    return xc * lax.rsqrt(var + LN_EPS) * g + b


def _pack_bf16_pair(a, b):
    ia = lax.bitcast_convert_type(a.astype(BF16).astype(F32), jnp.int32)
    ib = lax.bitcast_convert_type(b.astype(BF16).astype(F32), jnp.int32)
    return lax.shift_right_logical(ia, 16) | (ib & -65536)


def _unpack_bf16_pair(w):
    a = lax.bitcast_convert_type(lax.shift_left(w, 16), F32)
    b = lax.bitcast_convert_type(w & -65536, F32)
    return a, b


def _unpack_row(w_ref, k=None):
    parts = []
    for half in range(2):
        parts += list(_unpack_bf16_pair(w_ref[half] if k is None else w_ref[half, k]))
    return jnp.concatenate(parts, axis=1)


def _mix_kernel(x_ref, ya_ref, oc_ref, os_ref, ow_ref, g_ref, m_ref, ge_ref, wa_ref, wb_ref,
                wo_ref, lng_ref, lnb_ref, wr_ref, br_ref, tri_ref,
                h_ref, hw_ref, e_ref, r_ref, g_out_ref, cnt_ref):
    gb = g_ref[...].astype(BF16)
    yb = (_dot(gb, ge_ref[0]) * oc_ref[...].astype(F32)
          + _dot(gb, ge_ref[1]) * os_ref[...].astype(F32)
          + _dot(gb, ge_ref[2]) * ow_ref[...].astype(F32))
    ma = _dot(ya_ref[...], wa_ref[...])
    mb = _dot(yb.astype(BF16), wb_ref[...])
    merged = m_ref[:, 0:D_MODEL].astype(F32) * ma + m_ref[:, D_MODEL:2 * D_MODEL].astype(F32) * mb
    mix = _dot(merged.astype(BF16), wo_ref[...])
    h = _layer_norm(DEEPNORM_ALPHA * x_ref[...] + mix, lng_ref[...], lnb_ref[...])
    h_ref[...] = h
    hw_ref[0] = _pack_bf16_pair(h[:, 0:256], h[:, 256:512])
    hw_ref[1] = _pack_bf16_pair(h[:, 512:768], h[:, 768:1024])
    h_hi, h_lo = _split_bf16(h)
    logits = (_dot_nt(wr_ref[0], h_hi) + _dot_nt(wr_ref[0], h_lo) + _dot_nt(wr_ref[1], h_hi)
              + br_ref[...])
    row = lax.broadcasted_iota(jnp.int32, logits.shape, 0).astype(F32)
    vals, hits, idxs = [], [], []
    v = logits
    for _ in range(TOP_K):
        mx = jnp.max(v, axis=0, keepdims=True)
        idx = jnp.min(jnp.where(v == mx, row, 1e9), axis=0, keepdims=True)
        hit = row == idx
        vals.append(mx)
        hits.append(hit)
        idxs.append(idx)
        v = jnp.where(hit, -3e38, v)
    es = [jnp.exp(t - vals[0]) for t in vals]
    den = es[0] + es[1] + es[2] + es[3]

    @pl.when(pl.program_id(0) == 0)
    def _():
        cnt_ref[...] = jnp.zeros(cnt_ref.shape, F32)

    routed = sum(jnp.where(hit, 1.0, 0.0) for hit in hits)
    before = _dot(routed.astype(BF16), tri_ref[...]) + cnt_ref[:, 0:1]
    for k in range(TOP_K):
        e_ref[k:k + 1, :] = idxs[k].astype(jnp.int32)
        r_ref[k:k + 1, :] = jnp.sum(jnp.where(hits[k], before, 0.0), axis=0,
                                    keepdims=True).astype(jnp.int32)
        g_out_ref[k:k + 1, :] = es[k] / den
    cnt_ref[...] = cnt_ref[...] + jnp.sum(routed, axis=1, keepdims=True)


def _gate_expand_matrix():
    e = np.zeros((3, LANES, 512), np.float32)
    for c in range(3):
        for h in range(N_HEADS):
            e[c, 3 * h + c, h * HEAD_DIM:(h + 1) * HEAD_DIM] = 1.0
    return jnp.asarray(e, BF16)


def _mix(x2d, ya, oc, osl, ow, gates, merge, w_branch_a, w_branch_b, w_out, ln_g, ln_b,
         w_router, b_router):
    n_tok = x2d.shape[0]
    tm = MIX_TM
    row = lambda i: (i, 0)
    fix2 = lambda i: (0, 0)
    fix3 = lambda i: (0, 0, 0)
    wr_t = w_router.T
    wr_hi = wr_t.astype(BF16)
    wr = jnp.stack([wr_hi, (wr_t - wr_hi.astype(F32)).astype(BF16)])
    tri = jnp.asarray(np.triu(np.ones((tm, tm), np.float32), k=1), BF16)
    tok = lambda i: (0, i)
    return pl.pallas_call(
        _mix_kernel,
        grid=(n_tok // tm,),
        in_specs=[pl.BlockSpec((tm, D_MODEL), row),
                  pl.BlockSpec((tm, 512), row), pl.BlockSpec((tm, 512), row),
                  pl.BlockSpec((tm, 512), row), pl.BlockSpec((tm, 512), row),
                  pl.BlockSpec((tm, LANES), row), pl.BlockSpec((tm, 2 * D_MODEL), row),
                  pl.BlockSpec((3, LANES, 512), fix3),
                  pl.BlockSpec((512, D_MODEL), fix2), pl.BlockSpec((512, D_MODEL), fix2),
                  pl.BlockSpec((D_MODEL, D_MODEL), fix2),
                  pl.BlockSpec((1, D_MODEL), fix2), pl.BlockSpec((1, D_MODEL), fix2),
                  pl.BlockSpec((2, N_EXPERTS, D_MODEL), fix3),
                  pl.BlockSpec((N_EXPERTS, 1), fix2),
                  pl.BlockSpec((tm, tm), fix2)],
        out_specs=[pl.BlockSpec((tm, D_MODEL), row),
                   pl.BlockSpec((2, tm, 256), lambda i: (0, i, 0)),
                   pl.BlockSpec((TOP_K, tm), tok), pl.BlockSpec((TOP_K, tm), tok),
                   pl.BlockSpec((TOP_K, tm), tok),
                   pl.BlockSpec((N_EXPERTS, LANES), fix2)],
        out_shape=[jax.ShapeDtypeStruct((n_tok, D_MODEL), F32),
                   jax.ShapeDtypeStruct((2, n_tok, 256), jnp.int32),
                   jax.ShapeDtypeStruct((TOP_K, n_tok), jnp.int32),
                   jax.ShapeDtypeStruct((TOP_K, n_tok), jnp.int32),
                   jax.ShapeDtypeStruct((TOP_K, n_tok), F32),
                   jax.ShapeDtypeStruct((N_EXPERTS, LANES), F32)],
        compiler_params=_cparams(("arbitrary",)),
    )(x2d, ya, oc, osl, ow, gates, merge, _gate_expand_matrix(),
      w_branch_a.astype(BF16), w_branch_b.astype(BF16), w_out.astype(BF16),
      ln_g[None, :], ln_b[None, :], wr, b_router[:, None], tri)


def _sc_mesh():
    return plsc.VectorSubcoreMesh(core_axis_name="core", subcore_axis_name="subcore")


def _sc_scatter_rows(x, idx, n_out):
    n_tok = x.shape[1]
    n_win = n_tok // SC_WINDOW
    x2 = x.reshape(2 * n_tok, 256)

    @pl.kernel(out_type=jax.ShapeDtypeStruct((n_out, 256), x.dtype), mesh=_sc_mesh(),
               scratch_types=[])
    def scatter(x_hbm, i_hbm, o_hbm):
        def body(x_vmem, i_vmem):
            pltpu.sync_copy(x_vmem, o_hbm.at[i_vmem.at[0]])

        pltpu.emit_pipeline(
            body, grid=(2 * TOP_K * n_win,),
            in_specs=[pl.BlockSpec((SC_WINDOW, 256),
                                   lambda c: ((c // (TOP_K * n_win)) * n_win + c % n_win, 0)),
                      pl.BlockSpec((1, SC_WINDOW), lambda c: (0, c))],
            out_specs=[], core_axis_name=("core", "subcore"),
            dimension_semantics=(pltpu.PARALLEL,))(x_hbm, i_hbm)

    return scatter(x2, idx)


def _sc_gather_rows(y, idx):
    n_idx = idx.shape[1]

    @pl.kernel(out_type=jax.ShapeDtypeStruct((n_idx, 256), y.dtype), mesh=_sc_mesh())
    def gather(y_hbm, i_hbm, o_hbm):
        def body(i_vmem, o_vmem):
            pltpu.sync_copy(y_hbm.at[i_vmem.at[0]], o_vmem)

        pltpu.emit_pipeline(
            body, grid=(n_idx // SC_WINDOW,),
            in_specs=[pl.BlockSpec((1, SC_WINDOW), lambda c: (0, c))],
            out_specs=[pl.BlockSpec((SC_WINDOW, 256), lambda c: (c, 0))],
            core_axis_name=("core", "subcore"),
            dimension_semantics=(pltpu.PARALLEL,))(i_hbm, o_hbm)

    return gather(y, idx)


def _ffn_kernel(be_ref, nu_ref, r_ref, wgu_ref, bgu_ref, wd_ref, bd_ref, y_ref, wgu_s, wd_s):
    i = pl.program_id(0)

    @pl.when(i < nu_ref[0])
    def _():
        @pl.when((i == 0) | (be_ref[i] != be_ref[jnp.maximum(i - 1, 0)]))
        def _():
            wgu_s[...] = wgu_ref[0].astype(BF16)
            wd_s[...] = wd_ref[0].astype(BF16)

        x = _unpack_row(r_ref).astype(BF16)
        hcat = _dot(x, wgu_s[...]) + bgu_ref[0]
        glu = jnp.minimum(hcat[:, 0:D_FF], SWIGLU_LIMIT)
        lin = jnp.clip(hcat[:, D_FF:2 * D_FF], -SWIGLU_LIMIT, SWIGLU_LIMIT)
        act = glu * jax.nn.sigmoid(SWIGLU_ALPHA * glu) * (lin + 1.0)
        y = _dot(act.astype(BF16), wd_s[...]) + bd_ref[0]
        y_ref[0] = _pack_bf16_pair(y[:, 0:256], y[:, 256:512])
        y_ref[1] = _pack_bf16_pair(y[:, 512:768], y[:, 768:1024])


def _expert_ffn(rows, block_expert, n_used, w_gate_up, b_gate_up, w_down, b_down):
    n_slots = rows.shape[1]
    grid_spec = pltpu.PrefetchScalarGridSpec(
        num_scalar_prefetch=2,
        grid=(n_slots // MOE_BLOCK,),
        in_specs=[pl.BlockSpec((2, MOE_BLOCK, 256), lambda i, be, nu: (0, i, 0)),
                  pl.BlockSpec((1, D_MODEL, 2 * D_FF), lambda i, be, nu: (be[i], 0, 0)),
                  pl.BlockSpec((1, 1, 2 * D_FF), lambda i, be, nu: (be[i], 0, 0)),
                  pl.BlockSpec((1, D_FF, D_MODEL), lambda i, be, nu: (be[i], 0, 0)),
                  pl.BlockSpec((1, 1, D_MODEL), lambda i, be, nu: (be[i], 0, 0))],
        out_specs=pl.BlockSpec((2, MOE_BLOCK, 256), lambda i, be, nu: (0, i, 0)),
        scratch_shapes=[pltpu.VMEM((D_MODEL, 2 * D_FF), BF16), pltpu.VMEM((D_FF, D_MODEL), BF16)])
    return pl.pallas_call(
        _ffn_kernel,
        grid_spec=grid_spec,
        out_shape=jax.ShapeDtypeStruct((2, n_slots, 256), jnp.int32),
        compiler_params=_cparams(("arbitrary",)),
    )(block_expert, n_used, rows, w_gate_up, b_gate_up[:, None, :], w_down, b_down[:, None, :])


def _dispatch_plan(eidx, rank, counts, n_tok):
    n_slots = n_tok * TOP_K + N_EXPERTS * MOE_BLOCK
    cnt = counts[:, 0].astype(jnp.int32)
    padded = (cnt + (MOE_BLOCK - 1)) // MOE_BLOCK * MOE_BLOCK
    ends = jnp.cumsum(padded)
    base = ends - padded
    onehot = eidx[..., None] == jnp.arange(N_EXPERTS, dtype=jnp.int32)
    slot = jnp.sum(jnp.where(onehot, base, 0), axis=-1) + rank
    blk_start = jnp.arange(n_slots // MOE_BLOCK, dtype=jnp.int32) * MOE_BLOCK
    block_expert = jnp.sum((ends[None, :] <= blk_start[:, None]).astype(jnp.int32), axis=1)
    block_expert = jnp.minimum(block_expert, N_EXPERTS - 1)
    n_used = (ends[-1:] // MOE_BLOCK).astype(jnp.int32)
    idx = jnp.concatenate([slot, slot + n_slots], axis=0).reshape(1, 2 * TOP_K * n_tok)
    return n_slots, idx, block_expert, n_used


def _moe(hw, eidx, rank, counts, w_gate_up, b_gate_up, w_down, b_down):
    n_tok = hw.shape[1]
    n_slots, idx, block_expert, n_used = _dispatch_plan(eidx, rank, counts, n_tok)
    rows = _sc_scatter_rows(hw, idx, 2 * n_slots).reshape(2, n_slots, 256)
    y = _expert_ffn(rows, block_expert, n_used, w_gate_up, b_gate_up, w_down, b_down)
    yg = _sc_gather_rows(y.reshape(2 * n_slots, 256), idx)
    return yg.reshape(2, TOP_K, n_tok, 256)


def _combine_ln_kernel(h_ref, yg_ref, w_ref, g_ref, b_ref, o_ref):
    ffn = jnp.zeros(h_ref.shape, F32)
    for k in range(TOP_K):
        ffn = ffn + w_ref[:, k:k + 1] * _unpack_row(yg_ref, k)
    o_ref[...] = _layer_norm(DEEPNORM_ALPHA * h_ref[...] + ffn, g_ref[...], b_ref[...])


def _combine_ln(h, yg, w_tok, g, b):
    n_tok = h.shape[0]
    tm = 512
    row = lambda i: (i, 0)
    fix = lambda i: (0, 0)
    return pl.pallas_call(
        _combine_ln_kernel,
        grid=(n_tok // tm,),
        in_specs=[pl.BlockSpec((tm, D_MODEL), row),
                  pl.BlockSpec((2, TOP_K, tm, 256), lambda i: (0, 0, i, 0)),
                  pl.BlockSpec((tm, TOP_K), row),
                  pl.BlockSpec((1, D_MODEL), fix), pl.BlockSpec((1, D_MODEL), fix)],
        out_specs=pl.BlockSpec((tm, D_MODEL), row),
        out_shape=jax.ShapeDtypeStruct((n_tok, D_MODEL), F32),
        compiler_params=_cparams(("arbitrary",)),
    )(h, yg, w_tok, g[None, :], b[None, :])


def _layer(x2d, bsz, w_in, b_in, rel_bias, attn_sinks, cmp_pos_k, cmp_w1_k, cmp_w2_k, cmp_pos_v,
           cmp_w1_v, cmp_w2_v, w_branch_a, w_branch_b, w_out, ln1_g, ln1_b, w_router, b_router,
           w_gate_up, b_gate_up, w_down, b_down, ln2_g, ln2_b):
    w_packed, b_packed = _pack_in_weights(w_in, b_in)
    q_all, k_all, v_all, gates, merge, c_out = _in_projection(x2d, w_packed, b_packed)

    bias_a = rel_bias[:, :N_HEADS]
    bias_b = rel_bias[:, N_HEADS:]
    sinks = jnp.broadcast_to((attn_sinks * LOG2E).reshape(N_KV, 1, 4, 1), (N_KV, 1, 4, QB))
    sinks = sinks.reshape(N_KV, 1, 4 * QB)
    ya = _banded_attention(q_all, k_all, v_all, _band_bias(bias_a, A_WINDOW, 1), sinks,
                           q_col=0, kv_col=0, n_prev=1, n_sub=4)
    ow = _banded_attention(q_all, k_all, v_all, _band_bias(bias_b, B_WINDOW, 4), None,
                           q_col=1, kv_col=2, n_prev=4, n_sub=2)
    kv_cmp = _compress(c_out, bsz, cmp_pos_k, cmp_w1_k, cmp_w2_k, cmp_pos_v, cmp_w1_v, cmp_w2_v)
    oc, sel = _cmp_select(q_all, kv_cmp)
    osl = _selected_attention(q_all, k_all, v_all, sel, bias_b)

    h, hw, eidx, rank, gate_w, counts = _mix(x2d, ya, oc, osl, ow, gates, merge, w_branch_a,
                                             w_branch_b, w_out, ln1_g, ln1_b, w_router, b_router)
    yg = _moe(hw, eidx, rank, counts, w_gate_up, b_gate_up, w_down, b_down)
    return _combine_ln(h, yg, gate_w.T, ln2_g, ln2_b)


def kernel(x, w_in, b_in, rel_bias, attn_sinks, cmp_pos_k, cmp_w1_k, cmp_w2_k, cmp_pos_v, cmp_w1_v,
           cmp_w2_v, w_branch_a, w_branch_b, w_out, ln1_g, ln1_b, w_router, b_router, w_gate_up,
           b_gate_up, w_down, b_down, ln2_g, ln2_b):
    bsz, seq, d = x.shape
    assert seq == SEQ and d == D_MODEL
    h = x.reshape(bsz * seq, d)
    for l in range(w_in.shape[0]):
        h = _layer(h, bsz, w_in[l], b_in[l], rel_bias, attn_sinks[l], cmp_pos_k[l], cmp_w1_k[l],
                   cmp_w2_k[l], cmp_pos_v[l], cmp_w1_v[l], cmp_w2_v[l], w_branch_a[l],
                   w_branch_b[l], w_out[l], ln1_g[l], ln1_b[l], w_router[l], b_router[l],
                   w_gate_up[l], b_gate_up[l], w_down[l], b_down[l], ln2_g[l], ln2_b[l])
    return h.reshape(bsz, seq, d)
```

```python
import functools
import math

import jax
import jax.numpy as jnp
import numpy as np
from jax import lax
from jax.experimental import pallas as pl
from jax.experimental.pallas import tpu as pltpu
from jax.experimental.pallas import tpu_sc as plsc

F32 = jnp.float32
BF16 = jnp.bfloat16

D_MODEL = 1024
SEQ = 2048
HEAD_DIM = 64
N_HEADS = 8
N_KV = 2
A_WINDOW = 128
B_WINDOW = 512
CMP_LEN = 32
CMP_STRIDE = 16
CMP_HIDDEN = 128
SLC_LEN = 64
SLC_TOP = 8
SLC_LOCAL = 2
N_BUCKETS = 32
REL_MAX_DIST = 128
N_EXPERTS = 32
TOP_K = 4
D_FF = D_MODEL
SWIGLU_LIMIT = 7.0
SWIGLU_ALPHA = 1.702
LN_EPS = 1e-5
NEG_INF = -1e30
FORCED_SCORE = 1e30
DEEPNORM_ALPHA = 2.0 ** 0.25
LOG2E = math.log2(math.e)
Q_SCALE = HEAD_DIM ** -0.5 * LOG2E

LANES = 128
QB = 128
N_QB = SEQ // QB
SLC_QB = 256
N_SQB = SEQ // SLC_QB
N_CHUNK = SEQ // CMP_STRIDE
N_SLC = SEQ // SLC_LEN
VMEM_LIMIT = 56 * 1024 * 1024

IN_OFF = dict(qa=0, ka=512, va=640, qb=768, kbc=1280, vbc=1408, kbs=1536, vbs=1664,
              kbw=1792, vbw=1920, gate=2048, merge=2072)
PROJ_TM = 512
MOE_BLOCK = 512
SC_WINDOW = 128
MOE_PARTS = 2


def _cparams(sem):
    return pltpu.CompilerParams(dimension_semantics=sem, vmem_limit_bytes=VMEM_LIMIT)


def _dot(a, b):
    return jnp.dot(a, b, preferred_element_type=F32)


def _dot_nt(a, b):
    return lax.dot_general(a, b, (((1,), (1,)), ((), ())), preferred_element_type=F32)


def _dot_tn(a, b):
    return lax.dot_general(a, b, (((0,), (0,)), ((), ())), preferred_element_type=F32)


def _split_bf16(x):
    hi = x.astype(BF16)
    lo = (x - hi.astype(F32)).astype(BF16)
    return hi, lo


def _proj_kernel(x_ref, w_ref, b_ref, q_ref, k_ref, v_ref, g_ref, m_ref, c_ref):
    xb = x_ref[...].astype(BF16)

    def mm(c0, c1):
        return _dot(xb, w_ref[:, c0:c1]) + b_ref[:, c0:c1]

    for c in range(0, 1024, 512):
        q_ref[:, c:c + 512] = mm(c, c + 512).astype(BF16)
    k_ref[...] = mm(1024, 1792).astype(BF16)
    v_ref[...] = mm(1792, 2560).astype(BF16)
    g_ref[...] = jax.nn.sigmoid(mm(2560, 2688))
    for c in range(0, 2048, 512):
        m_ref[:, c:c + 512] = jax.nn.sigmoid(mm(2688 + c, 2688 + c + 512)).astype(BF16)
    c_ref[...] = mm(4736, 4992)


def _pack_in_weights(w_in, b_in):
    def cols(name, width):
        o = IN_OFF[name]
        return w_in[:, o:o + width], b_in[o:o + width]

    def dup_groups(name):
        w, b = cols(name, 128)
        ws, bs = [], []
        for g in range(N_KV):
            wg, bg = w[:, g * 64:(g + 1) * 64], b[g * 64:(g + 1) * 64]
            ws += [wg, wg]
            bs += [bg, bg]
        return jnp.concatenate(ws, axis=1), jnp.concatenate(bs)

    def scaled(name, width):
        w, b = cols(name, width)
        return w * Q_SCALE, b * Q_SCALE

    parts = [scaled('qa', 512), scaled('qb', 512),
             dup_groups('ka'), dup_groups('kbs'), dup_groups('kbw'),
             dup_groups('va'), dup_groups('vbs'), dup_groups('vbw')]
    wg, bg = cols('gate', 24)
    parts.append((jnp.pad(wg, ((0, 0), (0, 104))), jnp.pad(bg, (0, 104))))
    parts.append(cols('merge', 2048))
    parts.append(cols('kbc', 128))
    parts.append(cols('vbc', 128))
    w = jnp.concatenate([p[0] for p in parts], axis=1).astype(BF16)
    b = jnp.concatenate([p[1] for p in parts])[None, :]
    return w, b


def _in_projection(x2d, w_packed, b_packed):
    n_tok = x2d.shape[0]
    n_col = w_packed.shape[1]
    tm = PROJ_TM
    row = lambda i: (i, 0)
    fixed = lambda i: (0, 0)
    widths = (1024, 768, 768, 128, 2048, 256)
    dtypes = (BF16, BF16, BF16, F32, BF16, F32)
    return pl.pallas_call(
        _proj_kernel,
        grid=(n_tok // tm,),
        in_specs=[pl.BlockSpec((tm, D_MODEL), row),
                  pl.BlockSpec((D_MODEL, n_col), fixed),
                  pl.BlockSpec((1, n_col), fixed)],
        out_specs=[pl.BlockSpec((tm, w), row) for w in widths],
        out_shape=[jax.ShapeDtypeStruct((n_tok, w), dt) for w, dt in zip(widths, dtypes)],
        compiler_params=_cparams(("arbitrary",)),
    )(x2d, w_packed, b_packed)


def _bucket_np(rel):
    n = np.maximum(rel, 0)
    max_exact = N_BUCKETS // 2
    nf = np.maximum(n, 1).astype(np.float32)
    large = max_exact + (np.log(nf / max_exact) / math.log(REL_MAX_DIST / max_exact)
                         * (N_BUCKETS - max_exact)).astype(np.int32)
    large = np.minimum(large, N_BUCKETS - 1)
    return np.where(n < max_exact, n, large)


def _bias_tiles(bias_heads, rel, valid):
    n_var, n_q, n_k = rel.shape
    onehot = (jnp.asarray(_bucket_np(rel), jnp.int32)[..., None]
              == jnp.arange(N_BUCKETS, dtype=jnp.int32)).astype(F32)
    tab = jnp.einsum('vack,kh->vhca', onehot, bias_heads, precision=lax.Precision.HIGHEST)
    tab = jnp.where(jnp.asarray(np.swapaxes(valid, 1, 2))[:, None], tab * LOG2E, NEG_INF)
    tab = tab.reshape(n_var, N_KV, 4, n_k, n_q).transpose(0, 1, 3, 2, 4)
    return tab.reshape(n_var, N_KV, n_k, 4 * n_q)


def _band_bias(bias_heads, window, n_prev):
    w = (n_prev + 1) * QB
    v = np.arange(n_prev + 1)[:, None, None]
    rel = v * QB + np.arange(QB)[None, :, None] - np.arange(w)[None, None, :]
    return _bias_tiles(bias_heads, rel, (rel >= 0) & (rel < window))


def _slc_bias(bias_heads):
    d = np.arange(3)[:, None, None]
    rel = d * SLC_QB + np.arange(SLC_QB)[None, :, None] - np.arange(SLC_QB)[None, None, :]
    return _bias_tiles(bias_heads, rel, rel >= 0)


def _stack_heads(q_ref, rows, g, lo):
    parts = []
    for c in range(2):
        col = (2 * g + c) * LANES
        q2 = q_ref[rows, col:col + LANES]
        parts += [jnp.where(lo, q2, 0), jnp.where(lo, 0, q2)]
    return jnp.concatenate(parts, axis=0)


def _unstack_heads(o, o_ref, rows, g, lo, n):
    for c in range(2):
        col = (2 * g + c) * LANES
        odd = pltpu.roll(o[(2 * c + 1) * n:(2 * c + 2) * n], HEAD_DIM, axis=1)
        pair = jnp.where(lo, o[2 * c * n:(2 * c + 1) * n], odd)
        o_ref[rows, col:col + LANES] = pair.astype(BF16)


def _with_ones_lane(v, lane):
    return jnp.where(lane < HEAD_DIM, v, jnp.where(lane == HEAD_DIM, 1.0, 0.0).astype(v.dtype))


def _banded_kernel(*refs, n_prev, n_sub, has_sinks):
    if has_sinks:
        q_ref, k_ref, v_ref, bias_ref, sink_ref, o_ref = refs
    else:
        q_ref, k_ref, v_ref, bias_ref, o_ref = refs
    w = (n_prev + 1) * QB
    lane = lax.broadcasted_iota(jnp.int32, (1, LANES), 1)
    lo = lane < HEAD_DIM
    for u in range(n_sub):
        i = (pl.program_id(0) * n_sub + u) % N_QB
        start = pl.multiple_of(jnp.maximum(i - n_prev, 0) * QB, QB)
        var = jnp.minimum(i, n_prev)
        rows = slice(u * QB, (u + 1) * QB)
        for g in range(N_KV):
            kc = k_ref[pl.ds(start, w), g * LANES:(g + 1) * LANES]
            vc = _with_ones_lane(v_ref[pl.ds(start, w), g * LANES:(g + 1) * LANES], lane)
            st = _dot_nt(kc, _stack_heads(q_ref, rows, g, lo)) + bias_ref[var, g]
            m = jnp.max(st, axis=0, keepdims=True)
            if has_sinks:
                sk = sink_ref[g]
                m = jnp.maximum(m, sk)
            ot = _dot_tn(vc, jnp.exp2(st - m).astype(BF16))
            den = ot[HEAD_DIM:HEAD_DIM + 1, :]
            if has_sinks:
                den = den + jnp.exp2(sk - m)
            _unstack_heads((ot / den).T, o_ref, rows, g, lo, QB)


def _banded_attention(q_all, k_all, v_all, bias, sinks, q_col, kv_col, n_prev, n_sub):
    n_tok = q_all.shape[0]
    has_sinks = sinks is not None
    w = (n_prev + 1) * QB
    tq = QB * n_sub
    in_specs = [pl.BlockSpec((tq, 512), lambda gi: (gi, q_col)),
                pl.BlockSpec((SEQ, 256), lambda gi: (gi // (SEQ // tq), kv_col)),
                pl.BlockSpec((SEQ, 256), lambda gi: (gi // (SEQ // tq), kv_col)),
                pl.BlockSpec((n_prev + 1, N_KV, w, 4 * QB), lambda gi: (0, 0, 0, 0))]
    args = [q_all, k_all, v_all, bias]
    if has_sinks:
        in_specs.append(pl.BlockSpec((N_KV, 1, 4 * QB), lambda gi: (0, 0, 0)))
        args.append(sinks)
    return pl.pallas_call(
        functools.partial(_banded_kernel, n_prev=n_prev, n_sub=n_sub, has_sinks=has_sinks),
        grid=(n_tok // tq,),
        in_specs=in_specs,
        out_specs=pl.BlockSpec((tq, 512), lambda gi: (gi, 0)),
        out_shape=jax.ShapeDtypeStruct((n_tok, 512), BF16),
        compiler_params=_cparams(("arbitrary",)),
    )(*args)


def _compress_kernel(z_ref, pos_ref, w1_ref, w2_ref, o_ref):
    half = CMP_STRIDE * HEAD_DIM
    for kv in range(2):
        for g in range(N_KV):
            z = z_ref[0, 2 * kv + g]
            za_h, za_l = _split_bf16(z + pos_ref[kv, :, 0:half])
            zb_h, zb_l = _split_bf16(z + pos_ref[kv, :, half:2 * half])
            w1a_h, w1a_l = w1_ref[kv, 0, 0:half], w1_ref[kv, 1, 0:half]
            w1b_h, w1b_l = w1_ref[kv, 0, half:2 * half], w1_ref[kv, 1, half:2 * half]
            ha = _dot(za_h, w1a_h) + _dot(za_l, w1a_h) + _dot(za_h, w1a_l)
            hb = _dot(zb_h, w1b_h) + _dot(zb_l, w1b_h) + _dot(zb_h, w1b_l)
            h = ha + pltpu.roll(hb, N_CHUNK - 1, axis=0)
            a = jax.nn.gelu(h)
            a_h, a_l = _split_bf16(a)
            out = (_dot(a_h, w2_ref[kv, 0]) + _dot(a_l, w2_ref[kv, 0]) + _dot(a_h, w2_ref[kv, 1]))
            o_ref[0, 2 * kv + g] = out.astype(BF16)


def _compress(c_out, bsz, cmp_pos_k, cmp_w1_k, cmp_w2_k, cmp_pos_v, cmp_w1_v, cmp_w2_v):
    z = c_out.reshape(bsz, N_CHUNK, CMP_STRIDE, 4, HEAD_DIM)
    z = jnp.transpose(z, (0, 3, 1, 2, 4)).reshape(bsz, 4, N_CHUNK, CMP_STRIDE * HEAD_DIM)
    pos = jnp.stack([cmp_pos_k.reshape(1, -1), cmp_pos_v.reshape(1, -1)])

    def split(w):
        hi = w.astype(BF16)
        return jnp.stack([hi, (w - hi.astype(F32)).astype(BF16)])

    w1 = jnp.stack([split(cmp_w1_k), split(cmp_w1_v)])
    w2 = jnp.stack([split(jnp.concatenate([cmp_w2_k, cmp_w2_k], axis=1)),
                    split(jnp.concatenate([cmp_w2_v, cmp_w2_v], axis=1))])
    return pl.pallas_call(
        _compress_kernel,
        grid=(bsz,),
        in_specs=[pl.BlockSpec((1, 4, N_CHUNK, 1024), lambda b: (b, 0, 0, 0)),
                  pl.BlockSpec((2, 1, 2048), lambda b: (0, 0, 0)),
                  pl.BlockSpec((2, 2, 2048, CMP_HIDDEN), lambda b: (0, 0, 0, 0)),
                  pl.BlockSpec((2, 2, CMP_HIDDEN, LANES), lambda b: (0, 0, 0, 0))],
        out_specs=pl.BlockSpec((1, 4, N_CHUNK, LANES), lambda b: (b, 0, 0, 0)),
        out_shape=jax.ShapeDtypeStruct((bsz, 4, N_CHUNK, LANES), BF16),
        compiler_params=_cparams(("arbitrary",)),
    )(z, pos, w1, w2)


CMP_QB = 256


def _cmp_select_kernel(q_ref, kv_ref, ov_ref, o_ref, sel_ref):
    i = pl.program_id(0) % (SEQ // CMP_QB)
    lo = lax.broadcasted_iota(jnp.int32, (1, LANES), 1) < HEAD_DIM
    blk = lax.broadcasted_iota(jnp.int32, (LANES, 1), 0)
    t = i * CMP_QB + lax.broadcasted_iota(jnp.int32, (1, CMP_QB), 1)
    valid = (blk * CMP_STRIDE + (CMP_LEN - 1)) <= t
    valid4 = jnp.concatenate([valid] * 4, axis=1)
    any_valid = jnp.concatenate([t >= CMP_LEN - 1] * 4, axis=1)
    cur = lax.shift_right_logical(t, int(math.log2(SLC_LEN)))
    forced = (blk == 0) | ((blk <= cur) & (blk > cur - SLC_LOCAL))
    future = blk > cur
    blk_f = blk.astype(F32)
    rows = slice(0, CMP_QB)
    for g in range(N_KV):
        kc = kv_ref[0, g]
        vc = kv_ref[0, 2 + g]
        st = jnp.where(valid4, _dot_nt(kc, _stack_heads(q_ref, rows, g, lo)), NEG_INF)
        m = jnp.max(st, axis=0, keepdims=True)
        e = jnp.exp2(st - m)
        p = e / jnp.sum(e, axis=0, keepdims=True)
        p = jnp.where(any_valid, p, 0.0)
        ot = _dot_tn(vc, p.astype(BF16))
        _unstack_heads(ot.T, o_ref, rows, g, lo, CMP_QB)
        psum = (p[:, 0:CMP_QB] + p[:, CMP_QB:2 * CMP_QB]
                + p[:, 2 * CMP_QB:3 * CMP_QB] + p[:, 3 * CMP_QB:4 * CMP_QB])
        p_h, p_l = _split_bf16(psum)
        imp = _dot(ov_ref[...], p_h) + _dot(ov_ref[...], p_l)
        imp = jnp.where(forced, FORCED_SCORE, imp)
        imp = jnp.where(future, NEG_INF, imp)
        imp = jnp.where(blk < N_SLC, imp, -3e38)
        sel = jnp.zeros((LANES, CMP_QB), F32)
        for _ in range(SLC_TOP):
            mx = jnp.max(imp, axis=0, keepdims=True)
            idx = jnp.min(jnp.where(imp == mx, blk_f, 1e9), axis=0, keepdims=True)
            hit = blk_f == idx
            sel = jnp.where(hit, 1.0, sel)
            imp = jnp.where(hit, -3e38, imp)
        sel_ref[g] = sel.astype(BF16)


def _cmp_overlap_matrix():
    nc = (SEQ - CMP_LEN) // CMP_STRIDE + 1
    cs = np.arange(nc)[None, :] * CMP_STRIDE
    ss = np.arange(N_SLC)[:, None] * SLC_LEN
    ov = np.clip(np.minimum(cs + CMP_LEN, ss + SLC_LEN) - np.maximum(cs, ss), 0, None)
    out = np.zeros((LANES, LANES), np.float32)
    out[:N_SLC, :nc] = ov / CMP_LEN
    return jnp.asarray(out, BF16)


def _cmp_select(q_all, kv_cmp):
    n_tok = q_all.shape[0]
    nqb = SEQ // CMP_QB
    return pl.pallas_call(
        _cmp_select_kernel,
        grid=(n_tok // CMP_QB,),
        in_specs=[pl.BlockSpec((CMP_QB, 512), lambda gi: (gi, 1)),
                  pl.BlockSpec((1, 4, N_CHUNK, LANES), lambda gi: (gi // nqb, 0, 0, 0)),
                  pl.BlockSpec((LANES, LANES), lambda gi: (0, 0))],
        out_specs=[pl.BlockSpec((CMP_QB, 512), lambda gi: (gi, 0)),
                   pl.BlockSpec((N_KV, LANES, CMP_QB), lambda gi: (0, 0, gi))],
        out_shape=[jax.ShapeDtypeStruct((n_tok, 512), BF16),
                   jax.ShapeDtypeStruct((N_KV, LANES, n_tok), BF16)],
        compiler_params=_cparams(("arbitrary",)),
    )(q_all, kv_cmp, _cmp_overlap_matrix())


def _selected_kernel(q_ref, k_ref, v_ref, sel_ref, exp_ref, bias_ref, o_ref):
    i = pl.program_id(0) % N_SQB
    lo = lax.broadcasted_iota(jnp.int32, (1, LANES), 1) < HEAD_DIM
    rows = slice(0, SLC_QB)
    qs = [_stack_heads(q_ref, rows, g, lo) for g in range(N_KV)]

    def tile(j, carry):
        ks = pl.multiple_of(j * SLC_QB, SLC_QB)
        d = jnp.minimum(i - j, 2)
        out = []
        for g in range(N_KV):
            m, l, acc = carry[g]
            kt = k_ref[pl.ds(ks, SLC_QB), g * LANES:(g + 1) * LANES]
            vt = v_ref[pl.ds(ks, SLC_QB), g * LANES:(g + 1) * LANES]
            member = _dot(exp_ref[j], sel_ref[g])
            mask_add = (member - 1.0) * (-NEG_INF)
            st = (_dot_nt(kt, qs[g]) + bias_ref[d, g]) + jnp.concatenate([mask_add] * 4, axis=1)
            m_new = jnp.maximum(m, jnp.max(st, axis=0, keepdims=True))
            alpha = jnp.exp2(m - m_new)
            e = jnp.exp2(st - m_new)
            l = alpha * l + jnp.sum(e, axis=0, keepdims=True)
            acc = acc * alpha + _dot_tn(vt, e.astype(BF16))
            out.append((m_new, l, acc))
        return tuple(out)

    init = tuple((jnp.full((1, 4 * SLC_QB), -3e38, F32), jnp.zeros((1, 4 * SLC_QB), F32),
                  jnp.zeros((LANES, 4 * SLC_QB), F32)) for _ in range(N_KV))
    res = lax.fori_loop(0, i + 1, tile, init)
    for g in range(N_KV):
        _, l, acc = res[g]
        _unstack_heads((acc / l).T, o_ref, rows, g, lo, SLC_QB)


def _slc_expand_matrix():
    key_blk = (np.arange(N_SQB)[:, None, None] * SLC_QB + np.arange(SLC_QB)[None, :, None]) // SLC_LEN
    e = (np.arange(LANES)[None, None, :] == key_blk).astype(np.float32)
    return jnp.asarray(e, BF16)


def _selected_attention(q_all, k_all, v_all, sel, bias_heads):
    n_tok = q_all.shape[0]
    return pl.pallas_call(
        _selected_kernel,
        grid=(n_tok // SLC_QB,),
        in_specs=[pl.BlockSpec((SLC_QB, 512), lambda gi: (gi, 1)),
                  pl.BlockSpec((SEQ, 256), lambda gi: (gi // N_SQB, 1)),
                  pl.BlockSpec((SEQ, 256), lambda gi: (gi // N_SQB, 1)),
                  pl.BlockSpec((N_KV, LANES, SLC_QB), lambda gi: (0, 0, gi)),
                  pl.BlockSpec((N_SQB, SLC_QB, LANES), lambda gi: (0, 0, 0)),
                  pl.BlockSpec((3, N_KV, SLC_QB, 4 * SLC_QB), lambda gi: (0, 0, 0, 0))],
        out_specs=pl.BlockSpec((SLC_QB, 512), lambda gi: (gi, 0)),
        out_shape=jax.ShapeDtypeStruct((n_tok, 512), BF16),
        compiler_params=_cparams(("arbitrary",)),
    )(q_all, k_all, v_all, sel, _slc_expand_matrix(), _slc_bias(bias_heads))


MIX_TM = 512


def _layer_norm(x, g, b):
    mu = jnp.mean(x, axis=-1, keepdims=True)
    xc = x - mu
    var = jnp.mean(xc * xc, axis=-1, keepdims=True)
    return xc * lax.rsqrt(var + LN_EPS) * g + b


def _pack_bf16_pair(a, b):
    ia = lax.bitcast_convert_type(a.astype(BF16).astype(F32), jnp.int32)
    ib = lax.bitcast_convert_type(b.astype(BF16).astype(F32), jnp.int32)
    return lax.shift_right_logical(ia, 16) | (ib & -65536)


def _unpack_bf16_pair(w):
    a = lax.bitcast_convert_type(lax.shift_left(w, 16), F32)
    b = lax.bitcast_convert_type(w & -65536, F32)
    return a, b


def _unpack_row(w_ref, k=None):
    parts = []
    for half in range(2):
        parts += list(_unpack_bf16_pair(w_ref[half] if k is None else w_ref[half, k]))
    return jnp.concatenate(parts, axis=1)


def _mix_kernel(x_ref, ya_ref, oc_ref, os_ref, ow_ref, g_ref, m_ref, ge_ref, wa_ref, wb_ref,
                wo_ref, lng_ref, lnb_ref, wr_ref, br_ref, tri_ref,
                h_ref, hw_ref, e_ref, r_ref, g_out_ref, cnt_ref):
    gb = g_ref[...].astype(BF16)
    yb = (_dot(gb, ge_ref[0]) * oc_ref[...].astype(F32)
          + _dot(gb, ge_ref[1]) * os_ref[...].astype(F32)
          + _dot(gb, ge_ref[2]) * ow_ref[...].astype(F32))
    ma = _dot(ya_ref[...], wa_ref[...])
    mb = _dot(yb.astype(BF16), wb_ref[...])
    merged = m_ref[:, 0:D_MODEL].astype(F32) * ma + m_ref[:, D_MODEL:2 * D_MODEL].astype(F32) * mb
    mix = _dot(merged.astype(BF16), wo_ref[...])
    h = _layer_norm(DEEPNORM_ALPHA * x_ref[...] + mix, lng_ref[...], lnb_ref[...])
    h_ref[...] = h
    hw_ref[0] = _pack_bf16_pair(h[:, 0:256], h[:, 256:512])
    hw_ref[1] = _pack_bf16_pair(h[:, 512:768], h[:, 768:1024])
    h_hi, h_lo = _split_bf16(h)
    logits = (_dot_nt(wr_ref[0], h_hi) + _dot_nt(wr_ref[0], h_lo) + _dot_nt(wr_ref[1], h_hi)
              + br_ref[...])
    row = lax.broadcasted_iota(jnp.int32, logits.shape, 0).astype(F32)
    vals, hits, idxs = [], [], []
    v = logits
    for _ in range(TOP_K):
        mx = jnp.max(v, axis=0, keepdims=True)
        idx = jnp.min(jnp.where(v == mx, row, 1e9), axis=0, keepdims=True)
        hit = row == idx
        vals.append(mx)
        hits.append(hit)
        idxs.append(idx)
        v = jnp.where(hit, -3e38, v)
    es = [jnp.exp(t - vals[0]) for t in vals]
    den = es[0] + es[1] + es[2] + es[3]

    @pl.when(pl.program_id(0) == 0)
    def _():
        cnt_ref[...] = jnp.zeros(cnt_ref.shape, F32)

    routed = sum(jnp.where(hit, 1.0, 0.0) for hit in hits)
    before = _dot(routed.astype(BF16), tri_ref[...]) + cnt_ref[:, 0:1]
    for k in range(TOP_K):
        e_ref[k:k + 1, :] = idxs[k].astype(jnp.int32)
        r_ref[k:k + 1, :] = jnp.sum(jnp.where(hits[k], before, 0.0), axis=0,
                                    keepdims=True).astype(jnp.int32)
        g_out_ref[k:k + 1, :] = es[k] / den
    cnt_ref[...] = cnt_ref[...] + jnp.sum(routed, axis=1, keepdims=True)


def _gate_expand_matrix():
    e = np.zeros((3, LANES, 512), np.float32)
    for c in range(3):
        for h in range(N_HEADS):
            e[c, 3 * h + c, h * HEAD_DIM:(h + 1) * HEAD_DIM] = 1.0
    return jnp.asarray(e, BF16)


def _mix(x2d, ya, oc, osl, ow, gates, merge, w_branch_a, w_branch_b, w_out, ln_g, ln_b,
         w_router, b_router, part):
    tm = MIX_TM
    n_tok = x2d.shape[0] // MOE_PARTS
    off = part * (n_tok // tm)
    row_in = lambda i: (i + off, 0)
    row = lambda i: (i, 0)
    fix2 = lambda i: (0, 0)
    fix3 = lambda i: (0, 0, 0)
    wr_t = w_router.T
    wr_hi = wr_t.astype(BF16)
    wr = jnp.stack([wr_hi, (wr_t - wr_hi.astype(F32)).astype(BF16)])
    tri = jnp.asarray(np.triu(np.ones((tm, tm), np.float32), k=1), BF16)
    tok = lambda i: (0, i)
    return pl.pallas_call(
        _mix_kernel,
        grid=(n_tok // tm,),
        in_specs=[pl.BlockSpec((tm, D_MODEL), row_in),
                  pl.BlockSpec((tm, 512), row_in), pl.BlockSpec((tm, 512), row_in),
                  pl.BlockSpec((tm, 512), row_in), pl.BlockSpec((tm, 512), row_in),
                  pl.BlockSpec((tm, LANES), row_in), pl.BlockSpec((tm, 2 * D_MODEL), row_in),
                  pl.BlockSpec((3, LANES, 512), fix3),
                  pl.BlockSpec((512, D_MODEL), fix2), pl.BlockSpec((512, D_MODEL), fix2),
                  pl.BlockSpec((D_MODEL, D_MODEL), fix2),
                  pl.BlockSpec((1, D_MODEL), fix2), pl.BlockSpec((1, D_MODEL), fix2),
                  pl.BlockSpec((2, N_EXPERTS, D_MODEL), fix3),
                  pl.BlockSpec((N_EXPERTS, 1), fix2),
                  pl.BlockSpec((tm, tm), fix2)],
        out_specs=[pl.BlockSpec((tm, D_MODEL), row),
                   pl.BlockSpec((2, tm, 256), lambda i: (0, i, 0)),
                   pl.BlockSpec((TOP_K, tm), tok), pl.BlockSpec((TOP_K, tm), tok),
                   pl.BlockSpec((TOP_K, tm), tok),
                   pl.BlockSpec((N_EXPERTS, LANES), fix2)],
        out_shape=[jax.ShapeDtypeStruct((n_tok, D_MODEL), F32),
                   jax.ShapeDtypeStruct((2, n_tok, 256), jnp.int32),
                   jax.ShapeDtypeStruct((TOP_K, n_tok), jnp.int32),
                   jax.ShapeDtypeStruct((TOP_K, n_tok), jnp.int32),
                   jax.ShapeDtypeStruct((TOP_K, n_tok), F32),
                   jax.ShapeDtypeStruct((N_EXPERTS, LANES), F32)],
        compiler_params=_cparams(("arbitrary",)),
    )(x2d, ya, oc, osl, ow, gates, merge, _gate_expand_matrix(),
      w_branch_a.astype(BF16), w_branch_b.astype(BF16), w_out.astype(BF16),
      ln_g[None, :], ln_b[None, :], wr, b_router[:, None], tri)


def _sc_mesh():
    return plsc.VectorSubcoreMesh(core_axis_name="core", subcore_axis_name="subcore")


def _sc_scatter_rows(x, idx, n_out):
    n_tok = x.shape[1]
    n_win = n_tok // SC_WINDOW
    x2 = x.reshape(2 * n_tok, 256)

    @pl.kernel(out_type=jax.ShapeDtypeStruct((n_out, 256), x.dtype), mesh=_sc_mesh(),
               scratch_types=[])
    def scatter(x_hbm, i_hbm, o_hbm):
        def body(x_vmem, i_vmem):
            pltpu.sync_copy(x_vmem, o_hbm.at[i_vmem.at[0]])

        pltpu.emit_pipeline(
            body, grid=(2 * TOP_K * n_win,),
            in_specs=[pl.BlockSpec((SC_WINDOW, 256),
                                   lambda c: ((c // (TOP_K * n_win)) * n_win + c % n_win, 0)),
                      pl.BlockSpec((1, SC_WINDOW), lambda c: (0, c))],
            out_specs=[], core_axis_name=("core", "subcore"),
            dimension_semantics=(pltpu.PARALLEL,))(x_hbm, i_hbm)

    return scatter(x2, idx)


def _sc_gather_rows(y, idx):
    n_idx = idx.shape[1]

    @pl.kernel(out_type=jax.ShapeDtypeStruct((n_idx, 256), y.dtype), mesh=_sc_mesh())
    def gather(y_hbm, i_hbm, o_hbm):
        def body(i_vmem, o_vmem):
            pltpu.sync_copy(y_hbm.at[i_vmem.at[0]], o_vmem)

        pltpu.emit_pipeline(
            body, grid=(n_idx // SC_WINDOW,),
            in_specs=[pl.BlockSpec((1, SC_WINDOW), lambda c: (0, c))],
            out_specs=[pl.BlockSpec((SC_WINDOW, 256), lambda c: (c, 0))],
            core_axis_name=("core", "subcore"),
            dimension_semantics=(pltpu.PARALLEL,))(i_hbm, o_hbm)

    return gather(y, idx)


def _ffn_kernel(be_ref, nu_ref, r_ref, wgu_ref, bgu_ref, wd_ref, bd_ref, y_ref, wgu_s, wd_s):
    i = pl.program_id(0)

    @pl.when(i < nu_ref[0])
    def _():
        @pl.when((i == 0) | (be_ref[i] != be_ref[jnp.maximum(i - 1, 0)]))
        def _():
            wgu_s[...] = wgu_ref[0].astype(BF16)
            wd_s[...] = wd_ref[0].astype(BF16)

        x = _unpack_row(r_ref).astype(BF16)
        hcat = _dot(x, wgu_s[...]) + bgu_ref[0]
        glu = jnp.minimum(hcat[:, 0:D_FF], SWIGLU_LIMIT)
        lin = jnp.clip(hcat[:, D_FF:2 * D_FF], -SWIGLU_LIMIT, SWIGLU_LIMIT)
        act = glu * jax.nn.sigmoid(SWIGLU_ALPHA * glu) * (lin + 1.0)
        y = _dot(act.astype(BF16), wd_s[...]) + bd_ref[0]
        y_ref[0] = _pack_bf16_pair(y[:, 0:256], y[:, 256:512])
        y_ref[1] = _pack_bf16_pair(y[:, 512:768], y[:, 768:1024])


def _expert_ffn(rows, block_expert, n_used, w_gate_up, b_gate_up, w_down, b_down):
    n_slots = rows.shape[1]
    grid_spec = pltpu.PrefetchScalarGridSpec(
        num_scalar_prefetch=2,
        grid=(n_slots // MOE_BLOCK,),
        in_specs=[pl.BlockSpec((2, MOE_BLOCK, 256), lambda i, be, nu: (0, i, 0)),
                  pl.BlockSpec((1, D_MODEL, 2 * D_FF), lambda i, be, nu: (be[i], 0, 0)),
                  pl.BlockSpec((1, 1, 2 * D_FF), lambda i, be, nu: (be[i], 0, 0)),
                  pl.BlockSpec((1, D_FF, D_MODEL), lambda i, be, nu: (be[i], 0, 0)),
                  pl.BlockSpec((1, 1, D_MODEL), lambda i, be, nu: (be[i], 0, 0))],
        out_specs=pl.BlockSpec((2, MOE_BLOCK, 256), lambda i, be, nu: (0, i, 0)),
        scratch_shapes=[pltpu.VMEM((D_MODEL, 2 * D_FF), BF16), pltpu.VMEM((D_FF, D_MODEL), BF16)])
    return pl.pallas_call(
        _ffn_kernel,
        grid_spec=grid_spec,
        out_shape=jax.ShapeDtypeStruct((2, n_slots, 256), jnp.int32),
        compiler_params=_cparams(("arbitrary",)),
    )(block_expert, n_used, rows, w_gate_up, b_gate_up[:, None, :], w_down, b_down[:, None, :])


def _dispatch_plan(eidx, rank, counts, n_tok):
    n_slots = n_tok * TOP_K + N_EXPERTS * MOE_BLOCK
    cnt = counts[:, 0].astype(jnp.int32)
    padded = (cnt + (MOE_BLOCK - 1)) // MOE_BLOCK * MOE_BLOCK
    ends = jnp.cumsum(padded)
    base = ends - padded
    onehot = eidx[..., None] == jnp.arange(N_EXPERTS, dtype=jnp.int32)
    slot = jnp.sum(jnp.where(onehot, base, 0), axis=-1) + rank
    blk_start = jnp.arange(n_slots // MOE_BLOCK, dtype=jnp.int32) * MOE_BLOCK
    block_expert = jnp.sum((ends[None, :] <= blk_start[:, None]).astype(jnp.int32), axis=1)
    block_expert = jnp.minimum(block_expert, N_EXPERTS - 1)
    n_used = (ends[-1:] // MOE_BLOCK).astype(jnp.int32)
    idx = jnp.concatenate([slot, slot + n_slots], axis=0).reshape(1, 2 * TOP_K * n_tok)
    return n_slots, idx, block_expert, n_used


def _moe(hw, eidx, rank, counts, w_gate_up, b_gate_up, w_down, b_down):
    n_tok = hw.shape[1]
    n_slots, idx, block_expert, n_used = _dispatch_plan(eidx, rank, counts, n_tok)
    rows = _sc_scatter_rows(hw, idx, 2 * n_slots).reshape(2, n_slots, 256)
    y = _expert_ffn(rows, block_expert, n_used, w_gate_up, b_gate_up, w_down, b_down)
    yg = _sc_gather_rows(y.reshape(2 * n_slots, 256), idx)
    return yg.reshape(2, TOP_K, n_tok, 256)


def _combine_ln_kernel(h_ref, yg_ref, w_ref, g_ref, b_ref, *rest):
    o_ref = rest[-1]
    ffn = jnp.zeros(h_ref.shape, F32)
    for k in range(TOP_K):
        ffn = ffn + w_ref[:, k:k + 1] * _unpack_row(yg_ref, k)
    o_ref[...] = _layer_norm(DEEPNORM_ALPHA * h_ref[...] + ffn, g_ref[...], b_ref[...])


def _combine_ln(h, yg, w_tok, g, b, part, out_prev):
    n_part = h.shape[0]
    tm = 512
    off = part * (n_part // tm)
    row = lambda i: (i, 0)
    fix = lambda i: (0, 0)
    in_specs = [pl.BlockSpec((tm, D_MODEL), row),
                pl.BlockSpec((2, TOP_K, tm, 256), lambda i: (0, 0, i, 0)),
                pl.BlockSpec((tm, TOP_K), row),
                pl.BlockSpec((1, D_MODEL), fix), pl.BlockSpec((1, D_MODEL), fix)]
    args = [h, yg, w_tok, g[None, :], b[None, :]]
    aliases = {}
    if out_prev is not None:
        in_specs.append(pl.BlockSpec(memory_space=pl.ANY))
        args.append(out_prev)
        aliases = {len(args) - 1: 0}
    return pl.pallas_call(
        _combine_ln_kernel,
        grid=(n_part // tm,),
        in_specs=in_specs,
        out_specs=pl.BlockSpec((tm, D_MODEL), lambda i: (i + off, 0)),
        out_shape=jax.ShapeDtypeStruct((MOE_PARTS * n_part, D_MODEL), F32),
        input_output_aliases=aliases,
        compiler_params=_cparams(("arbitrary",)),
    )(*args)


def _layer(x2d, bsz, w_in, b_in, rel_bias, attn_sinks, cmp_pos_k, cmp_w1_k, cmp_w2_k, cmp_pos_v,
           cmp_w1_v, cmp_w2_v, w_branch_a, w_branch_b, w_out, ln1_g, ln1_b, w_router, b_router,
           w_gate_up, b_gate_up, w_down, b_down, ln2_g, ln2_b):
    w_packed, b_packed = _pack_in_weights(w_in, b_in)
    q_all, k_all, v_all, gates, merge, c_out = _in_projection(x2d, w_packed, b_packed)

    bias_a = rel_bias[:, :N_HEADS]
    bias_b = rel_bias[:, N_HEADS:]
    sinks = jnp.broadcast_to((attn_sinks * LOG2E).reshape(N_KV, 1, 4, 1), (N_KV, 1, 4, QB))
    sinks = sinks.reshape(N_KV, 1, 4 * QB)
    ya = _banded_attention(q_all, k_all, v_all, _band_bias(bias_a, A_WINDOW, 1), sinks,
                           q_col=0, kv_col=0, n_prev=1, n_sub=4)
    ow = _banded_attention(q_all, k_all, v_all, _band_bias(bias_b, B_WINDOW, 4), None,
                           q_col=1, kv_col=2, n_prev=4, n_sub=2)
    kv_cmp = _compress(c_out, bsz, cmp_pos_k, cmp_w1_k, cmp_w2_k, cmp_pos_v, cmp_w1_v, cmp_w2_v)
    oc, sel = _cmp_select(q_all, kv_cmp)
    osl = _selected_attention(q_all, k_all, v_all, sel, bias_b)

    parts = []
    for p in range(MOE_PARTS):
        h, hw, eidx, rank, gate_w, counts = _mix(x2d, ya, oc, osl, ow, gates, merge, w_branch_a,
                                                 w_branch_b, w_out, ln1_g, ln1_b, w_router,
                                                 b_router, p)
        yg = _moe(hw, eidx, rank, counts, w_gate_up, b_gate_up, w_down, b_down)
        parts.append((h, yg, gate_w.T))
    out = None
    for p, (h, yg, w_tok) in enumerate(parts):
        out = _combine_ln(h, yg, w_tok, ln2_g, ln2_b, p, out)
    return out


def kernel(x, w_in, b_in, rel_bias, attn_sinks, cmp_pos_k, cmp_w1_k, cmp_w2_k, cmp_pos_v, cmp_w1_v,
           cmp_w2_v, w_branch_a, w_branch_b, w_out, ln1_g, ln1_b, w_router, b_router, w_gate_up,
           b_gate_up, w_down, b_down, ln2_g, ln2_b):
    bsz, seq, d = x.shape
    assert seq == SEQ and d == D_MODEL
    h = x.reshape(bsz * seq, d)
    for l in range(w_in.shape[0]):
        h = _layer(h, bsz, w_in[l], b_in[l], rel_bias, attn_sinks[l], cmp_pos_k[l], cmp_w1_k[l],
                   cmp_w2_k[l], cmp_pos_v[l], cmp_w1_v[l], cmp_w2_v[l], w_branch_a[l],
                   w_branch_b[l], w_out[l], ln1_g[l], ln1_b[l], w_router[l], b_router[l],
                   w_gate_up[l], b_gate_up[l], w_down[l], b_down[l], ln2_g[l], ln2_b[l])
    return h.reshape(bsz, seq, d)
```

```python
import functools
import math

import jax
import jax.numpy as jnp
import numpy as np
from jax import lax
from jax.experimental import pallas as pl
from jax.experimental.pallas import tpu as pltpu
from jax.experimental.pallas import tpu_sc as plsc

F32 = jnp.float32
BF16 = jnp.bfloat16

D_MODEL = 1024
SEQ = 2048
HEAD_DIM = 64
N_HEADS = 8
N_KV = 2
A_WINDOW = 128
B_WINDOW = 512
CMP_LEN = 32
CMP_STRIDE = 16
CMP_HIDDEN = 128
SLC_LEN = 64
SLC_TOP = 8
SLC_LOCAL = 2
N_BUCKETS = 32
REL_MAX_DIST = 128
N_EXPERTS = 32
TOP_K = 4
D_FF = D_MODEL
SWIGLU_LIMIT = 7.0
SWIGLU_ALPHA = 1.702
LN_EPS = 1e-5
NEG_INF = -1e30
FORCED_SCORE = 1e30
DEEPNORM_ALPHA = 2.0 ** 0.25
LOG2E = math.log2(math.e)
Q_SCALE = HEAD_DIM ** -0.5 * LOG2E

LANES = 128
QB = 128
N_QB = SEQ // QB
SLC_QB = 256
N_SQB = SEQ // SLC_QB
N_CHUNK = SEQ // CMP_STRIDE
N_SLC = SEQ // SLC_LEN
VMEM_LIMIT = 56 * 1024 * 1024

IN_OFF = dict(qa=0, ka=512, va=640, qb=768, kbc=1280, vbc=1408, kbs=1536, vbs=1664,
              kbw=1792, vbw=1920, gate=2048, merge=2072)
PROJ_TM = 512
MOE_BLOCK = 512
SC_WINDOW = 128
MOE_PARTS = 2


def _cparams(sem):
    return pltpu.CompilerParams(dimension_semantics=sem, vmem_limit_bytes=VMEM_LIMIT)


def _dot(a, b):
    return jnp.dot(a, b, preferred_element_type=F32)


def _dot_nt(a, b):
    return lax.dot_general(a, b, (((1,), (1,)), ((), ())), preferred_element_type=F32)


def _dot_tn(a, b):
    return lax.dot_general(a, b, (((0,), (0,)), ((), ())), preferred_element_type=F32)


def _split_bf16(x):
    hi = x.astype(BF16)
    lo = (x - hi.astype(F32)).astype(BF16)
    return hi, lo


def _proj_kernel(x_ref, w_ref, b_ref, q_ref, k_ref, v_ref, g_ref, m_ref, c_ref):
    xb = x_ref[...].astype(BF16)

    def mm(c0, c1):
        return _dot(xb, w_ref[:, c0:c1]) + b_ref[:, c0:c1]

    for c in range(0, 1024, 512):
        q_ref[:, c:c + 512] = mm(c, c + 512).astype(BF16)
    k_ref[...] = mm(1024, 1792).astype(BF16)
    v_ref[...] = mm(1792, 2560).astype(BF16)
    g_ref[...] = jax.nn.sigmoid(mm(2560, 2688))
    for c in range(0, 2048, 512):
        m_ref[:, c:c + 512] = jax.nn.sigmoid(mm(2688 + c, 2688 + c + 512)).astype(BF16)
    c_ref[...] = mm(4736, 4992)


def _pack_in_weights(w_in, b_in):
    def cols(name, width):
        o = IN_OFF[name]
        return w_in[:, o:o + width], b_in[o:o + width]

    def dup_groups(name):
        w, b = cols(name, 128)
        ws, bs = [], []
        for g in range(N_KV):
            wg, bg = w[:, g * 64:(g + 1) * 64], b[g * 64:(g + 1) * 64]
            ws += [wg, wg]
            bs += [bg, bg]
        return jnp.concatenate(ws, axis=1), jnp.concatenate(bs)

    def scaled(name, width):
        w, b = cols(name, width)
        return w * Q_SCALE, b * Q_SCALE

    parts = [scaled('qa', 512), scaled('qb', 512),
             dup_groups('ka'), dup_groups('kbs'), dup_groups('kbw'),
             dup_groups('va'), dup_groups('vbs'), dup_groups('vbw')]
    wg, bg = cols('gate', 24)
    parts.append((jnp.pad(wg, ((0, 0), (0, 104))), jnp.pad(bg, (0, 104))))
    parts.append(cols('merge', 2048))
    parts.append(cols('kbc', 128))
    parts.append(cols('vbc', 128))
    w = jnp.concatenate([p[0] for p in parts], axis=1).astype(BF16)
    b = jnp.concatenate([p[1] for p in parts])[None, :]
    return w, b


def _in_projection(x2d, w_packed, b_packed):
    n_tok = x2d.shape[0]
    n_col = w_packed.shape[1]
    tm = PROJ_TM
    row = lambda i: (i, 0)
    fixed = lambda i: (0, 0)
    widths = (1024, 768, 768, 128, 2048, 256)
    dtypes = (BF16, BF16, BF16, F32, BF16, F32)
    return pl.pallas_call(
        _proj_kernel,
        grid=(n_tok // tm,),
        in_specs=[pl.BlockSpec((tm, D_MODEL), row),
                  pl.BlockSpec((D_MODEL, n_col), fixed),
                  pl.BlockSpec((1, n_col), fixed)],
        out_specs=[pl.BlockSpec((tm, w), row) for w in widths],
        out_shape=[jax.ShapeDtypeStruct((n_tok, w), dt) for w, dt in zip(widths, dtypes)],
        compiler_params=_cparams(("arbitrary",)),
    )(x2d, w_packed, b_packed)


def _bucket_np(rel):
    n = np.maximum(rel, 0)
    max_exact = N_BUCKETS // 2
    nf = np.maximum(n, 1).astype(np.float32)
    large = max_exact + (np.log(nf / max_exact) / math.log(REL_MAX_DIST / max_exact)
                         * (N_BUCKETS - max_exact)).astype(np.int32)
    large = np.minimum(large, N_BUCKETS - 1)
    return np.where(n < max_exact, n, large)


def _bias_tiles(bias_heads, rel, valid):
    n_var, n_q, n_k = rel.shape
    onehot = (jnp.asarray(_bucket_np(rel), jnp.int32)[..., None]
              == jnp.arange(N_BUCKETS, dtype=jnp.int32)).astype(F32)
    tab = jnp.einsum('vack,kh->vhca', onehot, bias_heads, precision=lax.Precision.HIGHEST)
    tab = jnp.where(jnp.asarray(np.swapaxes(valid, 1, 2))[:, None], tab * LOG2E, NEG_INF)
    tab = tab.reshape(n_var, N_KV, 4, n_k, n_q).transpose(0, 1, 3, 2, 4)
    return tab.reshape(n_var, N_KV, n_k, 4 * n_q)


def _band_bias(bias_heads, window, n_prev):
    w = (n_prev + 1) * QB
    v = np.arange(n_prev + 1)[:, None, None]
    rel = v * QB + np.arange(QB)[None, :, None] - np.arange(w)[None, None, :]
    return _bias_tiles(bias_heads, rel, (rel >= 0) & (rel < window))


def _slc_bias(bias_heads):
    d = np.arange(3)[:, None, None]
    rel = d * SLC_QB + np.arange(SLC_QB)[None, :, None] - np.arange(SLC_QB)[None, None, :]
    return _bias_tiles(bias_heads, rel, rel >= 0)


def _stack_heads(q_ref, rows, g, lo):
    parts = []
    for c in range(2):
        col = (2 * g + c) * LANES
        q2 = q_ref[rows, col:col + LANES]
        parts += [jnp.where(lo, q2, 0), jnp.where(lo, 0, q2)]
    return jnp.concatenate(parts, axis=0)


def _unstack_heads(o, o_ref, rows, g, lo, n):
    for c in range(2):
        col = (2 * g + c) * LANES
        odd = pltpu.roll(o[(2 * c + 1) * n:(2 * c + 2) * n], HEAD_DIM, axis=1)
        pair = jnp.where(lo, o[2 * c * n:(2 * c + 1) * n], odd)
        o_ref[rows, col:col + LANES] = pair.astype(BF16)


def _with_ones_lane(v, lane):
    return jnp.where(lane < HEAD_DIM, v, jnp.where(lane == HEAD_DIM, 1.0, 0.0).astype(v.dtype))


def _banded_kernel(*refs, n_prev, n_sub, has_sinks):
    if has_sinks:
        q_ref, k_ref, v_ref, bias_ref, sink_ref, o_ref = refs
    else:
        q_ref, k_ref, v_ref, bias_ref, o_ref = refs
    w = (n_prev + 1) * QB
    lane = lax.broadcasted_iota(jnp.int32, (1, LANES), 1)
    lo = lane < HEAD_DIM
    for u in range(n_sub):
        i = (pl.program_id(0) * n_sub + u) % N_QB
        start = pl.multiple_of(jnp.maximum(i - n_prev, 0) * QB, QB)
        var = jnp.minimum(i, n_prev)
        rows = slice(u * QB, (u + 1) * QB)
        for g in range(N_KV):
            kc = k_ref[pl.ds(start, w), g * LANES:(g + 1) * LANES]
            vc = _with_ones_lane(v_ref[pl.ds(start, w), g * LANES:(g + 1) * LANES], lane)
            st = _dot_nt(kc, _stack_heads(q_ref, rows, g, lo)) + bias_ref[var, g]
            m = jnp.max(st, axis=0, keepdims=True)
            if has_sinks:
                sk = sink_ref[g]
                m = jnp.maximum(m, sk)
            ot = _dot_tn(vc, jnp.exp2(st - m).astype(BF16))
            den = ot[HEAD_DIM:HEAD_DIM + 1, :]
            if has_sinks:
                den = den + jnp.exp2(sk - m)
            _unstack_heads((ot / den).T, o_ref, rows, g, lo, QB)


def _banded_attention(q_all, k_all, v_all, bias, sinks, q_col, kv_col, n_prev, n_sub):
    n_tok = q_all.shape[0]
    has_sinks = sinks is not None
    w = (n_prev + 1) * QB
    tq = QB * n_sub
    in_specs = [pl.BlockSpec((tq, 512), lambda gi: (gi, q_col)),
                pl.BlockSpec((SEQ, 256), lambda gi: (gi // (SEQ // tq), kv_col)),
                pl.BlockSpec((SEQ, 256), lambda gi: (gi // (SEQ // tq), kv_col)),
                pl.BlockSpec((n_prev + 1, N_KV, w, 4 * QB), lambda gi: (0, 0, 0, 0))]
    args = [q_all, k_all, v_all, bias]
    if has_sinks:
        in_specs.append(pl.BlockSpec((N_KV, 1, 4 * QB), lambda gi: (0, 0, 0)))
        args.append(sinks)
    return pl.pallas_call(
        functools.partial(_banded_kernel, n_prev=n_prev, n_sub=n_sub, has_sinks=has_sinks),
        grid=(n_tok // tq,),
        in_specs=in_specs,
        out_specs=pl.BlockSpec((tq, 512), lambda gi: (gi, 0)),
        out_shape=jax.ShapeDtypeStruct((n_tok, 512), BF16),
        compiler_params=_cparams(("arbitrary",)),
    )(*args)


def _compress_kernel(z_ref, pos_ref, w1_ref, w2_ref, o_ref):
    half = CMP_STRIDE * HEAD_DIM
    for kv in range(2):
        for g in range(N_KV):
            z = z_ref[0, 2 * kv + g]
            za_h, za_l = _split_bf16(z + pos_ref[kv, :, 0:half])
            zb_h, zb_l = _split_bf16(z + pos_ref[kv, :, half:2 * half])
            w1a_h, w1a_l = w1_ref[kv, 0, 0:half], w1_ref[kv, 1, 0:half]
            w1b_h, w1b_l = w1_ref[kv, 0, half:2 * half], w1_ref[kv, 1, half:2 * half]
            ha = _dot(za_h, w1a_h) + _dot(za_l, w1a_h) + _dot(za_h, w1a_l)
            hb = _dot(zb_h, w1b_h) + _dot(zb_l, w1b_h) + _dot(zb_h, w1b_l)
            h = ha + pltpu.roll(hb, N_CHUNK - 1, axis=0)
            a = jax.nn.gelu(h)
            a_h, a_l = _split_bf16(a)
            out = (_dot(a_h, w2_ref[kv, 0]) + _dot(a_l, w2_ref[kv, 0]) + _dot(a_h, w2_ref[kv, 1]))
            o_ref[0, 2 * kv + g] = out.astype(BF16)


def _compress(c_out, bsz, cmp_pos_k, cmp_w1_k, cmp_w2_k, cmp_pos_v, cmp_w1_v, cmp_w2_v):
    z = c_out.reshape(bsz, N_CHUNK, CMP_STRIDE, 4, HEAD_DIM)
    z = jnp.transpose(z, (0, 3, 1, 2, 4)).reshape(bsz, 4, N_CHUNK, CMP_STRIDE * HEAD_DIM)
    pos = jnp.stack([cmp_pos_k.reshape(1, -1), cmp_pos_v.reshape(1, -1)])

    def split(w):
        hi = w.astype(BF16)
        return jnp.stack([hi, (w - hi.astype(F32)).astype(BF16)])

    w1 = jnp.stack([split(cmp_w1_k), split(cmp_w1_v)])
    w2 = jnp.stack([split(jnp.concatenate([cmp_w2_k, cmp_w2_k], axis=1)),
                    split(jnp.concatenate([cmp_w2_v, cmp_w2_v], axis=1))])
    return pl.pallas_call(
        _compress_kernel,
        grid=(bsz,),
        in_specs=[pl.BlockSpec((1, 4, N_CHUNK, 1024), lambda b: (b, 0, 0, 0)),
                  pl.BlockSpec((2, 1, 2048), lambda b: (0, 0, 0)),
                  pl.BlockSpec((2, 2, 2048, CMP_HIDDEN), lambda b: (0, 0, 0, 0)),
                  pl.BlockSpec((2, 2, CMP_HIDDEN, LANES), lambda b: (0, 0, 0, 0))],
        out_specs=pl.BlockSpec((1, 4, N_CHUNK, LANES), lambda b: (b, 0, 0, 0)),
        out_shape=jax.ShapeDtypeStruct((bsz, 4, N_CHUNK, LANES), BF16),
        compiler_params=_cparams(("arbitrary",)),
    )(z, pos, w1, w2)


CMP_QB = 256


def _cmp_select_kernel(q_ref, kv_ref, ov_ref, o_ref, sel_ref):
    i = pl.program_id(0) % (SEQ // CMP_QB)
    lo = lax.broadcasted_iota(jnp.int32, (1, LANES), 1) < HEAD_DIM
    blk = lax.broadcasted_iota(jnp.int32, (LANES, 1), 0)
    t = i * CMP_QB + lax.broadcasted_iota(jnp.int32, (1, CMP_QB), 1)
    valid = (blk * CMP_STRIDE + (CMP_LEN - 1)) <= t
    valid4 = jnp.concatenate([valid] * 4, axis=1)
    any_valid = jnp.concatenate([t >= CMP_LEN - 1] * 4, axis=1)
    cur = lax.shift_right_logical(t, int(math.log2(SLC_LEN)))
    forced = (blk == 0) | ((blk <= cur) & (blk > cur - SLC_LOCAL))
    future = blk > cur
    blk_f = blk.astype(F32)
    rows = slice(0, CMP_QB)
    for g in range(N_KV):
        kc = kv_ref[0, g]
        vc = kv_ref[0, 2 + g]
        st = jnp.where(valid4, _dot_nt(kc, _stack_heads(q_ref, rows, g, lo)), NEG_INF)
        m = jnp.max(st, axis=0, keepdims=True)
        e = jnp.exp2(st - m)
        p = e / jnp.sum(e, axis=0, keepdims=True)
        p = jnp.where(any_valid, p, 0.0)
        ot = _dot_tn(vc, p.astype(BF16))
        _unstack_heads(ot.T, o_ref, rows, g, lo, CMP_QB)
        psum = (p[:, 0:CMP_QB] + p[:, CMP_QB:2 * CMP_QB]
                + p[:, 2 * CMP_QB:3 * CMP_QB] + p[:, 3 * CMP_QB:4 * CMP_QB])
        p_h, p_l = _split_bf16(psum)
        imp = _dot(ov_ref[...], p_h) + _dot(ov_ref[...], p_l)
        imp = jnp.where(forced, FORCED_SCORE, imp)
        imp = jnp.where(future, NEG_INF, imp)
        imp = jnp.where(blk < N_SLC, imp, -3e38)
        sel = jnp.zeros((LANES, CMP_QB), F32)
        for _ in range(SLC_TOP):
            mx = jnp.max(imp, axis=0, keepdims=True)
            idx = jnp.min(jnp.where(imp == mx, blk_f, 1e9), axis=0, keepdims=True)
            hit = blk_f == idx
            sel = jnp.where(hit, 1.0, sel)
            imp = jnp.where(hit, -3e38, imp)
        sel_ref[g] = sel.astype(BF16)


def _cmp_overlap_matrix():
    nc = (SEQ - CMP_LEN) // CMP_STRIDE + 1
    cs = np.arange(nc)[None, :] * CMP_STRIDE
    ss = np.arange(N_SLC)[:, None] * SLC_LEN
    ov = np.clip(np.minimum(cs + CMP_LEN, ss + SLC_LEN) - np.maximum(cs, ss), 0, None)
    out = np.zeros((LANES, LANES), np.float32)
    out[:N_SLC, :nc] = ov / CMP_LEN
    return jnp.asarray(out, BF16)


def _cmp_select(q_all, kv_cmp):
    n_tok = q_all.shape[0]
    nqb = SEQ // CMP_QB
    return pl.pallas_call(
        _cmp_select_kernel,
        grid=(n_tok // CMP_QB,),
        in_specs=[pl.BlockSpec((CMP_QB, 512), lambda gi: (gi, 1)),
                  pl.BlockSpec((1, 4, N_CHUNK, LANES), lambda gi: (gi // nqb, 0, 0, 0)),
                  pl.BlockSpec((LANES, LANES), lambda gi: (0, 0))],
        out_specs=[pl.BlockSpec((CMP_QB, 512), lambda gi: (gi, 0)),
                   pl.BlockSpec((N_KV, LANES, CMP_QB), lambda gi: (0, 0, gi))],
        out_shape=[jax.ShapeDtypeStruct((n_tok, 512), BF16),
                   jax.ShapeDtypeStruct((N_KV, LANES, n_tok), BF16)],
        compiler_params=_cparams(("arbitrary",)),
    )(q_all, kv_cmp, _cmp_overlap_matrix())


def _selected_kernel(q_ref, k_ref, v_ref, sel_ref, exp_ref, bias_ref, o_ref):
    i = pl.program_id(0) % N_SQB
    lo = lax.broadcasted_iota(jnp.int32, (1, LANES), 1) < HEAD_DIM
    rows = slice(0, SLC_QB)
    qs = [_stack_heads(q_ref, rows, g, lo) for g in range(N_KV)]

    def tile(j, carry):
        ks = pl.multiple_of(j * SLC_QB, SLC_QB)
        d = jnp.minimum(i - j, 2)
        out = []
        for g in range(N_KV):
            m, l, acc = carry[g]
            kt = k_ref[pl.ds(ks, SLC_QB), g * LANES:(g + 1) * LANES]
            vt = v_ref[pl.ds(ks, SLC_QB), g * LANES:(g + 1) * LANES]
            member = _dot(exp_ref[j], sel_ref[g])
            mask_add = (member - 1.0) * (-NEG_INF)
            st = (_dot_nt(kt, qs[g]) + bias_ref[d, g]) + jnp.concatenate([mask_add] * 4, axis=1)
            m_new = jnp.maximum(m, jnp.max(st, axis=0, keepdims=True))
            alpha = jnp.exp2(m - m_new)
            e = jnp.exp2(st - m_new)
            l = alpha * l + jnp.sum(e, axis=0, keepdims=True)
            acc = acc * alpha + _dot_tn(vt, e.astype(BF16))
            out.append((m_new, l, acc))
        return tuple(out)

    init = tuple((jnp.full((1, 4 * SLC_QB), -3e38, F32), jnp.zeros((1, 4 * SLC_QB), F32),
                  jnp.zeros((LANES, 4 * SLC_QB), F32)) for _ in range(N_KV))
    res = lax.fori_loop(0, i + 1, tile, init)
    for g in range(N_KV):
        _, l, acc = res[g]
        _unstack_heads((acc / l).T, o_ref, rows, g, lo, SLC_QB)


def _slc_expand_matrix():
    key_blk = (np.arange(N_SQB)[:, None, None] * SLC_QB + np.arange(SLC_QB)[None, :, None]) // SLC_LEN
    e = (np.arange(LANES)[None, None, :] == key_blk).astype(np.float32)
    return jnp.asarray(e, BF16)


def _selected_attention(q_all, k_all, v_all, sel, bias_heads):
    n_tok = q_all.shape[0]
    return pl.pallas_call(
        _selected_kernel,
        grid=(n_tok // SLC_QB,),
        in_specs=[pl.BlockSpec((SLC_QB, 512), lambda gi: (gi, 1)),
                  pl.BlockSpec((SEQ, 256), lambda gi: (gi // N_SQB, 1)),
                  pl.BlockSpec((SEQ, 256), lambda gi: (gi // N_SQB, 1)),
                  pl.BlockSpec((N_KV, LANES, SLC_QB), lambda gi: (0, 0, gi)),
                  pl.BlockSpec((N_SQB, SLC_QB, LANES), lambda gi: (0, 0, 0)),
                  pl.BlockSpec((3, N_KV, SLC_QB, 4 * SLC_QB), lambda gi: (0, 0, 0, 0))],
        out_specs=pl.BlockSpec((SLC_QB, 512), lambda gi: (gi, 0)),
        out_shape=jax.ShapeDtypeStruct((n_tok, 512), BF16),
        compiler_params=_cparams(("arbitrary",)),
    )(q_all, k_all, v_all, sel, _slc_expand_matrix(), _slc_bias(bias_heads))


MIX_TM = 512
MIX_ROWS = 256


def _layer_norm(x, g, b):
    mu = jnp.mean(x, axis=-1, keepdims=True)
    xc = x - mu
    var = jnp.mean(xc * xc, axis=-1, keepdims=True)
    return xc * lax.rsqrt(var + LN_EPS) * g + b


def _pack_bf16_pair(a, b):
    ia = lax.bitcast_convert_type(a.astype(BF16).astype(F32), jnp.int32)
    ib = lax.bitcast_convert_type(b.astype(BF16).astype(F32), jnp.int32)
    return lax.shift_right_logical(ia, 16) | (ib & -65536)


def _unpack_bf16_pair(w):
    a = lax.bitcast_convert_type(lax.shift_left(w, 16), F32)
    b = lax.bitcast_convert_type(w & -65536, F32)
    return a, b


def _unpack_row(w_ref, k=None):
    parts = []
    for half in range(2):
        parts += list(_unpack_bf16_pair(w_ref[half] if k is None else w_ref[half, k]))
    return jnp.concatenate(parts, axis=1)


def _mix_kernel(x_ref, ya_ref, oc_ref, os_ref, ow_ref, g_ref, m_ref, ge_ref, wa_ref, wb_ref,
                wo_ref, lng_ref, lnb_ref, wr_ref, br_ref, tri_ref,
                h_ref, hw_ref, e_ref, r_ref, g_out_ref, cnt_ref):
    @pl.when(pl.program_id(0) == 0)
    def _():
        cnt_ref[...] = jnp.zeros(cnt_ref.shape, F32)

    for u in range(MIX_TM // MIX_ROWS):
        rs = slice(u * MIX_ROWS, (u + 1) * MIX_ROWS)
        gb = g_ref[rs, :].astype(BF16)
        yb = (_dot(gb, ge_ref[0]).astype(BF16) * oc_ref[rs, :]
              + _dot(gb, ge_ref[1]).astype(BF16) * os_ref[rs, :]
              + _dot(gb, ge_ref[2]).astype(BF16) * ow_ref[rs, :])
        ma = _dot(ya_ref[rs, :], wa_ref[...])
        mb = _dot(yb, wb_ref[...])
        merged = (m_ref[rs, 0:D_MODEL] * ma.astype(BF16)
                  + m_ref[rs, D_MODEL:2 * D_MODEL] * mb.astype(BF16))
        mix = _dot(merged, wo_ref[...])
        h = _layer_norm(DEEPNORM_ALPHA * x_ref[rs, :] + mix, lng_ref[...], lnb_ref[...])
        h_ref[rs, :] = h
        hw_ref[0, rs, :] = _pack_bf16_pair(h[:, 0:256], h[:, 256:512])
        hw_ref[1, rs, :] = _pack_bf16_pair(h[:, 512:768], h[:, 768:1024])
        h_hi, h_lo = _split_bf16(h)
        logits = (_dot_nt(wr_ref[0], h_hi) + _dot_nt(wr_ref[0], h_lo) + _dot_nt(wr_ref[1], h_hi)
                  + br_ref[...])
        row = lax.broadcasted_iota(jnp.int32, logits.shape, 0).astype(F32)
        vals, hits, idxs = [], [], []
        v = logits
        for _ in range(TOP_K):
            mx = jnp.max(v, axis=0, keepdims=True)
            idx = jnp.min(jnp.where(v == mx, row, 1e9), axis=0, keepdims=True)
            hit = row == idx
            vals.append(mx)
            hits.append(hit)
            idxs.append(idx)
            v = jnp.where(hit, -3e38, v)
        es = [jnp.exp(t - vals[0]) for t in vals]
        den = es[0] + es[1] + es[2] + es[3]
        routed = sum(jnp.where(hit, 1.0, 0.0) for hit in hits)
        before = _dot(routed.astype(BF16), tri_ref[...]) + cnt_ref[:, 0:1]
        for k in range(TOP_K):
            e_ref[k:k + 1, rs] = idxs[k].astype(jnp.int32)
            r_ref[k:k + 1, rs] = jnp.sum(jnp.where(hits[k], before, 0.0), axis=0,
                                         keepdims=True).astype(jnp.int32)
            g_out_ref[k:k + 1, rs] = es[k] / den
        cnt_ref[...] = cnt_ref[...] + jnp.sum(routed, axis=1, keepdims=True)


def _gate_expand_matrix():
    e = np.zeros((3, LANES, 512), np.float32)
    for c in range(3):
        for h in range(N_HEADS):
            e[c, 3 * h + c, h * HEAD_DIM:(h + 1) * HEAD_DIM] = 1.0
    return jnp.asarray(e, BF16)


def _mix(x2d, ya, oc, osl, ow, gates, merge, w_branch_a, w_branch_b, w_out, ln_g, ln_b,
         w_router, b_router, part):
    tm = MIX_TM
    n_tok = x2d.shape[0] // MOE_PARTS
    off = part * (n_tok // tm)
    row_in = lambda i: (i + off, 0)
    row = lambda i: (i, 0)
    fix2 = lambda i: (0, 0)
    fix3 = lambda i: (0, 0, 0)
    wr_t = w_router.T
    wr_hi = wr_t.astype(BF16)
    wr = jnp.stack([wr_hi, (wr_t - wr_hi.astype(F32)).astype(BF16)])
    tri = jnp.asarray(np.triu(np.ones((MIX_ROWS, MIX_ROWS), np.float32), k=1), BF16)
    tok = lambda i: (0, i)
    return pl.pallas_call(
        _mix_kernel,
        grid=(n_tok // tm,),
        in_specs=[pl.BlockSpec((tm, D_MODEL), row_in),
                  pl.BlockSpec((tm, 512), row_in), pl.BlockSpec((tm, 512), row_in),
                  pl.BlockSpec((tm, 512), row_in), pl.BlockSpec((tm, 512), row_in),
                  pl.BlockSpec((tm, LANES), row_in), pl.BlockSpec((tm, 2 * D_MODEL), row_in),
                  pl.BlockSpec((3, LANES, 512), fix3),
                  pl.BlockSpec((512, D_MODEL), fix2), pl.BlockSpec((512, D_MODEL), fix2),
                  pl.BlockSpec((D_MODEL, D_MODEL), fix2),
                  pl.BlockSpec((1, D_MODEL), fix2), pl.BlockSpec((1, D_MODEL), fix2),
                  pl.BlockSpec((2, N_EXPERTS, D_MODEL), fix3),
                  pl.BlockSpec((N_EXPERTS, 1), fix2),
                  pl.BlockSpec((MIX_ROWS, MIX_ROWS), fix2)],
        out_specs=[pl.BlockSpec((tm, D_MODEL), row),
                   pl.BlockSpec((2, tm, 256), lambda i: (0, i, 0)),
                   pl.BlockSpec((TOP_K, tm), tok), pl.BlockSpec((TOP_K, tm), tok),
                   pl.BlockSpec((TOP_K, tm), tok),
                   pl.BlockSpec((N_EXPERTS, LANES), fix2)],
        out_shape=[jax.ShapeDtypeStruct((n_tok, D_MODEL), F32),
                   jax.ShapeDtypeStruct((2, n_tok, 256), jnp.int32),
                   jax.ShapeDtypeStruct((TOP_K, n_tok), jnp.int32),
                   jax.ShapeDtypeStruct((TOP_K, n_tok), jnp.int32),
                   jax.ShapeDtypeStruct((TOP_K, n_tok), F32),
                   jax.ShapeDtypeStruct((N_EXPERTS, LANES), F32)],
        compiler_params=_cparams(("arbitrary",)),
    )(x2d, ya, oc, osl, ow, gates, merge, _gate_expand_matrix(),
      w_branch_a.astype(BF16), w_branch_b.astype(BF16), w_out.astype(BF16),
      ln_g[None, :], ln_b[None, :], wr, b_router[:, None], tri)


def _sc_mesh():
    return plsc.VectorSubcoreMesh(core_axis_name="core", subcore_axis_name="subcore")


def _sc_scatter_rows(x, idx, n_out):
    n_tok = x.shape[1]
    n_win = n_tok // SC_WINDOW
    x2 = x.reshape(2 * n_tok, 256)

    @pl.kernel(out_type=jax.ShapeDtypeStruct((n_out, 256), x.dtype), mesh=_sc_mesh(),
               scratch_types=[])
    def scatter(x_hbm, i_hbm, o_hbm):
        def body(x_vmem, i_vmem):
            pltpu.sync_copy(x_vmem, o_hbm.at[i_vmem.at[0]])

        pltpu.emit_pipeline(
            body, grid=(2 * TOP_K * n_win,),
            in_specs=[pl.BlockSpec((SC_WINDOW, 256),
                                   lambda c: ((c // (TOP_K * n_win)) * n_win + c % n_win, 0)),
                      pl.BlockSpec((1, SC_WINDOW), lambda c: (0, c))],
            out_specs=[], core_axis_name=("core", "subcore"),
            dimension_semantics=(pltpu.PARALLEL,))(x_hbm, i_hbm)

    return scatter(x2, idx)


def _sc_gather_rows(y, idx):
    n_idx = idx.shape[1]

    @pl.kernel(out_type=jax.ShapeDtypeStruct((n_idx, 256), y.dtype), mesh=_sc_mesh())
    def gather(y_hbm, i_hbm, o_hbm):
        def body(i_vmem, o_vmem):
            pltpu.sync_copy(y_hbm.at[i_vmem.at[0]], o_vmem)

        pltpu.emit_pipeline(
            body, grid=(n_idx // SC_WINDOW,),
            in_specs=[pl.BlockSpec((1, SC_WINDOW), lambda c: (0, c))],
            out_specs=[pl.BlockSpec((SC_WINDOW, 256), lambda c: (c, 0))],
            core_axis_name=("core", "subcore"),
            dimension_semantics=(pltpu.PARALLEL,))(i_hbm, o_hbm)

    return gather(y, idx)


def _ffn_kernel(be_ref, nu_ref, r_ref, wgu_ref, bgu_ref, wd_ref, bd_ref, y_ref, wgu_s, wd_s):
    i = pl.program_id(0)

    @pl.when(i < nu_ref[0])
    def _():
        @pl.when((i == 0) | (be_ref[i] != be_ref[jnp.maximum(i - 1, 0)]))
        def _():
            wgu_s[...] = wgu_ref[0].astype(BF16)
            wd_s[...] = wd_ref[0].astype(BF16)

        x = _unpack_row(r_ref).astype(BF16)
        hcat = _dot(x, wgu_s[...]) + bgu_ref[0]
        glu = jnp.minimum(hcat[:, 0:D_FF], SWIGLU_LIMIT)
        lin = jnp.clip(hcat[:, D_FF:2 * D_FF], -SWIGLU_LIMIT, SWIGLU_LIMIT)
        act = glu * jax.nn.sigmoid(SWIGLU_ALPHA * glu) * (lin + 1.0)
        y = _dot(act.astype(BF16), wd_s[...]) + bd_ref[0]
        y_ref[0] = _pack_bf16_pair(y[:, 0:256], y[:, 256:512])
        y_ref[1] = _pack_bf16_pair(y[:, 512:768], y[:, 768:1024])


def _expert_ffn(rows, block_expert, n_used, w_gate_up, b_gate_up, w_down, b_down):
    n_slots = rows.shape[1]
    grid_spec = pltpu.PrefetchScalarGridSpec(
        num_scalar_prefetch=2,
        grid=(n_slots // MOE_BLOCK,),
        in_specs=[pl.BlockSpec((2, MOE_BLOCK, 256), lambda i, be, nu: (0, i, 0)),
                  pl.BlockSpec((1, D_MODEL, 2 * D_FF), lambda i, be, nu: (be[i], 0, 0)),
                  pl.BlockSpec((1, 1, 2 * D_FF), lambda i, be, nu: (be[i], 0, 0)),
                  pl.BlockSpec((1, D_FF, D_MODEL), lambda i, be, nu: (be[i], 0, 0)),
                  pl.BlockSpec((1, 1, D_MODEL), lambda i, be, nu: (be[i], 0, 0))],
        out_specs=pl.BlockSpec((2, MOE_BLOCK, 256), lambda i, be, nu: (0, i, 0)),
        scratch_shapes=[pltpu.VMEM((D_MODEL, 2 * D_FF), BF16), pltpu.VMEM((D_FF, D_MODEL), BF16)])
    return pl.pallas_call(
        _ffn_kernel,
        grid_spec=grid_spec,
        out_shape=jax.ShapeDtypeStruct((2, n_slots, 256), jnp.int32),
        compiler_params=_cparams(("arbitrary",)),
    )(block_expert, n_used, rows, w_gate_up, b_gate_up[:, None, :], w_down, b_down[:, None, :])


def _dispatch_plan(eidx, rank, counts, n_tok):
    n_slots = n_tok * TOP_K + N_EXPERTS * MOE_BLOCK
    cnt = counts[:, 0].astype(jnp.int32)
    padded = (cnt + (MOE_BLOCK - 1)) // MOE_BLOCK * MOE_BLOCK
    ends = jnp.cumsum(padded)
    base = ends - padded
    onehot = eidx[..., None] == jnp.arange(N_EXPERTS, dtype=jnp.int32)
    slot = jnp.sum(jnp.where(onehot, base, 0), axis=-1) + rank
    blk_start = jnp.arange(n_slots // MOE_BLOCK, dtype=jnp.int32) * MOE_BLOCK
    block_expert = jnp.sum((ends[None, :] <= blk_start[:, None]).astype(jnp.int32), axis=1)
    block_expert = jnp.minimum(block_expert, N_EXPERTS - 1)
    n_used = (ends[-1:] // MOE_BLOCK).astype(jnp.int32)
    idx = jnp.concatenate([slot, slot + n_slots], axis=0).reshape(1, 2 * TOP_K * n_tok)
    return n_slots, idx, block_expert, n_used


def _moe(hw, eidx, rank, counts, w_gate_up, b_gate_up, w_down, b_down):
    n_tok = hw.shape[1]
    n_slots, idx, block_expert, n_used = _dispatch_plan(eidx, rank, counts, n_tok)
    rows = _sc_scatter_rows(hw, idx, 2 * n_slots).reshape(2, n_slots, 256)
    y = _expert_ffn(rows, block_expert, n_used, w_gate_up, b_gate_up, w_down, b_down)
    yg = _sc_gather_rows(y.reshape(2 * n_slots, 256), idx)
    return yg.reshape(2, TOP_K, n_tok, 256)


def _combine_ln_kernel(h_ref, yg_ref, w_ref, g_ref, b_ref, *rest):
    o_ref = rest[-1]
    ffn = jnp.zeros(h_ref.shape, F32)
    for k in range(TOP_K):
        ffn = ffn + w_ref[:, k:k + 1] * _unpack_row(yg_ref, k)
    o_ref[...] = _layer_norm(DEEPNORM_ALPHA * h_ref[...] + ffn, g_ref[...], b_ref[...])


def _combine_ln(h, yg, w_tok, g, b, part, out_prev):
    n_part = h.shape[0]
    tm = 512
    off = part * (n_part // tm)
    row = lambda i: (i, 0)
    fix = lambda i: (0, 0)
    in_specs = [pl.BlockSpec((tm, D_MODEL), row),
                pl.BlockSpec((2, TOP_K, tm, 256), lambda i: (0, 0, i, 0)),
                pl.BlockSpec((tm, TOP_K), row),
                pl.BlockSpec((1, D_MODEL), fix), pl.BlockSpec((1, D_MODEL), fix)]
    args = [h, yg, w_tok, g[None, :], b[None, :]]
    aliases = {}
    if out_prev is not None:
        in_specs.append(pl.BlockSpec(memory_space=pl.ANY))
        args.append(out_prev)
        aliases = {len(args) - 1: 0}
    return pl.pallas_call(
        _combine_ln_kernel,
        grid=(n_part // tm,),
        in_specs=in_specs,
        out_specs=pl.BlockSpec((tm, D_MODEL), lambda i: (i + off, 0)),
        out_shape=jax.ShapeDtypeStruct((MOE_PARTS * n_part, D_MODEL), F32),
        input_output_aliases=aliases,
        compiler_params=_cparams(("arbitrary",)),
    )(*args)


def _layer(x2d, bsz, w_in, b_in, rel_bias, attn_sinks, cmp_pos_k, cmp_w1_k, cmp_w2_k, cmp_pos_v,
           cmp_w1_v, cmp_w2_v, w_branch_a, w_branch_b, w_out, ln1_g, ln1_b, w_router, b_router,
           w_gate_up, b_gate_up, w_down, b_down, ln2_g, ln2_b):
    w_packed, b_packed = _pack_in_weights(w_in, b_in)
    q_all, k_all, v_all, gates, merge, c_out = _in_projection(x2d, w_packed, b_packed)

    bias_a = rel_bias[:, :N_HEADS]
    bias_b = rel_bias[:, N_HEADS:]
    sinks = jnp.broadcast_to((attn_sinks * LOG2E).reshape(N_KV, 1, 4, 1), (N_KV, 1, 4, QB))
    sinks = sinks.reshape(N_KV, 1, 4 * QB)
    ya = _banded_attention(q_all, k_all, v_all, _band_bias(bias_a, A_WINDOW, 1), sinks,
                           q_col=0, kv_col=0, n_prev=1, n_sub=4)
    ow = _banded_attention(q_all, k_all, v_all, _band_bias(bias_b, B_WINDOW, 4), None,
                           q_col=1, kv_col=2, n_prev=4, n_sub=4)
    kv_cmp = _compress(c_out, bsz, cmp_pos_k, cmp_w1_k, cmp_w2_k, cmp_pos_v, cmp_w1_v, cmp_w2_v)
    oc, sel = _cmp_select(q_all, kv_cmp)
    osl = _selected_attention(q_all, k_all, v_all, sel, bias_b)

    parts = []
    for p in range(MOE_PARTS):
        h, hw, eidx, rank, gate_w, counts = _mix(x2d, ya, oc, osl, ow, gates, merge, w_branch_a,
                                                 w_branch_b, w_out, ln1_g, ln1_b, w_router,
                                                 b_router, p)
        yg = _moe(hw, eidx, rank, counts, w_gate_up, b_gate_up, w_down, b_down)
        parts.append((h, yg, gate_w.T))
    out = None
    for p, (h, yg, w_tok) in enumerate(parts):
        out = _combine_ln(h, yg, w_tok, ln2_g, ln2_b, p, out)
    return out


def kernel(x, w_in, b_in, rel_bias, attn_sinks, cmp_pos_k, cmp_w1_k, cmp_w2_k, cmp_pos_v, cmp_w1_v,
           cmp_w2_v, w_branch_a, w_branch_b, w_out, ln1_g, ln1_b, w_router, b_router, w_gate_up,
           b_gate_up, w_down, b_down, ln2_g, ln2_b):
    bsz, seq, d = x.shape
    assert seq == SEQ and d == D_MODEL
    h = x.reshape(bsz * seq, d)
    for l in range(w_in.shape[0]):
        h = _layer(h, bsz, w_in[l], b_in[l], rel_bias, attn_sinks[l], cmp_pos_k[l], cmp_w1_k[l],
                   cmp_w2_k[l], cmp_pos_v[l], cmp_w1_v[l], cmp_w2_v[l], w_branch_a[l],
                   w_branch_b[l], w_out[l], ln1_g[l], ln1_b[l], w_router[l], b_router[l],
                   w_gate_up[l], b_gate_up[l], w_down[l], b_down[l], ln2_g[l], ln2_b[l])
    return h.reshape(bsz, seq, d)
```

```python
import functools
import math

import jax
import jax.numpy as jnp
import numpy as np
from jax import lax
from jax.experimental import pallas as pl
from jax.experimental.pallas import tpu as pltpu
from jax.experimental.pallas import tpu_sc as plsc

F32 = jnp.float32
BF16 = jnp.bfloat16

D_MODEL = 1024
SEQ = 2048
HEAD_DIM = 64
N_HEADS = 8
N_KV = 2
A_WINDOW = 128
B_WINDOW = 512
CMP_LEN = 32
CMP_STRIDE = 16
CMP_HIDDEN = 128
SLC_LEN = 64
SLC_TOP = 8
SLC_LOCAL = 2
N_BUCKETS = 32
REL_MAX_DIST = 128
N_EXPERTS = 32
TOP_K = 4
D_FF = D_MODEL
SWIGLU_LIMIT = 7.0
SWIGLU_ALPHA = 1.702
LN_EPS = 1e-5
NEG_INF = -1e30
FORCED_SCORE = 1e30
DEEPNORM_ALPHA = 2.0 ** 0.25
LOG2E = math.log2(math.e)
Q_SCALE = HEAD_DIM ** -0.5 * LOG2E

LANES = 128
QB = 128
N_QB = SEQ // QB
SLC_QB = 256
N_SQB = SEQ // SLC_QB
N_CHUNK = SEQ // CMP_STRIDE
N_SLC = SEQ // SLC_LEN
VMEM_LIMIT = 56 * 1024 * 1024

IN_OFF = dict(qa=0, ka=512, va=640, qb=768, kbc=1280, vbc=1408, kbs=1536, vbs=1664,
              kbw=1792, vbw=1920, gate=2048, merge=2072)
PROJ_TM = 512
MOE_BLOCK = 512
SC_WINDOW = 128
MOE_PARTS = 2


def _cparams(sem):
    return pltpu.CompilerParams(dimension_semantics=sem, vmem_limit_bytes=VMEM_LIMIT)


def _dot(a, b):
    return jnp.dot(a, b, preferred_element_type=F32)


def _dot_nt(a, b):
    return lax.dot_general(a, b, (((1,), (1,)), ((), ())), preferred_element_type=F32)


def _dot_tn(a, b):
    return lax.dot_general(a, b, (((0,), (0,)), ((), ())), preferred_element_type=F32)


def _split_bf16(x):
    hi = x.astype(BF16)
    lo = (x - hi.astype(F32)).astype(BF16)
    return hi, lo


def _proj_kernel(x_ref, w_ref, b_ref, q_ref, k_ref, v_ref, g_ref, m_ref, c_ref):
    xb = x_ref[...].astype(BF16)

    def mm(c0, c1):
        return _dot(xb, w_ref[:, c0:c1]) + b_ref[:, c0:c1]

    for c in range(0, 1024, 512):
        q_ref[:, c:c + 512] = mm(c, c + 512).astype(BF16)
    k_ref[...] = mm(1024, 1792).astype(BF16)
    v_ref[...] = mm(1792, 2560).astype(BF16)
    g_ref[...] = jax.nn.sigmoid(mm(2560, 2688))
    for c in range(0, 2048, 512):
        m_ref[:, c:c + 512] = jax.nn.sigmoid(mm(2688 + c, 2688 + c + 512)).astype(BF16)
    c_ref[...] = mm(4736, 4992)


def _pack_in_weights(w_in, b_in):
    def cols(name, width):
        o = IN_OFF[name]
        return w_in[:, o:o + width], b_in[o:o + width]

    def dup_groups(name):
        w, b = cols(name, 128)
        ws, bs = [], []
        for g in range(N_KV):
            wg, bg = w[:, g * 64:(g + 1) * 64], b[g * 64:(g + 1) * 64]
            ws += [wg, wg]
            bs += [bg, bg]
        return jnp.concatenate(ws, axis=1), jnp.concatenate(bs)

    def scaled(name, width):
        w, b = cols(name, width)
        return w * Q_SCALE, b * Q_SCALE

    parts = [scaled('qa', 512), scaled('qb', 512),
             dup_groups('ka'), dup_groups('kbs'), dup_groups('kbw'),
             dup_groups('va'), dup_groups('vbs'), dup_groups('vbw')]
    wg, bg = cols('gate', 24)
    parts.append((jnp.pad(wg, ((0, 0), (0, 104))), jnp.pad(bg, (0, 104))))
    parts.append(cols('merge', 2048))
    parts.append(cols('kbc', 128))
    parts.append(cols('vbc', 128))
    w = jnp.concatenate([p[0] for p in parts], axis=1).astype(BF16)
    b = jnp.concatenate([p[1] for p in parts])[None, :]
    return w, b


def _in_projection(x2d, w_packed, b_packed):
    n_tok = x2d.shape[0]
    n_col = w_packed.shape[1]
    tm = PROJ_TM
    row = lambda i: (i, 0)
    fixed = lambda i: (0, 0)
    widths = (1024, 768, 768, 128, 2048, 256)
    dtypes = (BF16, BF16, BF16, F32, BF16, F32)
    return pl.pallas_call(
        _proj_kernel,
        grid=(n_tok // tm,),
        in_specs=[pl.BlockSpec((tm, D_MODEL), row),
                  pl.BlockSpec((D_MODEL, n_col), fixed),
                  pl.BlockSpec((1, n_col), fixed)],
        out_specs=[pl.BlockSpec((tm, w), row) for w in widths],
        out_shape=[jax.ShapeDtypeStruct((n_tok, w), dt) for w, dt in zip(widths, dtypes)],
        compiler_params=_cparams(("arbitrary",)),
    )(x2d, w_packed, b_packed)


def _bucket_np(rel):
    n = np.maximum(rel, 0)
    max_exact = N_BUCKETS // 2
    nf = np.maximum(n, 1).astype(np.float32)
    large = max_exact + (np.log(nf / max_exact) / math.log(REL_MAX_DIST / max_exact)
                         * (N_BUCKETS - max_exact)).astype(np.int32)
    large = np.minimum(large, N_BUCKETS - 1)
    return np.where(n < max_exact, n, large)


def _bias_tiles(bias_heads, rel, valid):
    n_var, n_q, n_k = rel.shape
    onehot = (jnp.asarray(_bucket_np(rel), jnp.int32)[..., None]
              == jnp.arange(N_BUCKETS, dtype=jnp.int32)).astype(F32)
    tab = jnp.einsum('vack,kh->vhca', onehot, bias_heads, precision=lax.Precision.HIGHEST)
    tab = jnp.where(jnp.asarray(np.swapaxes(valid, 1, 2))[:, None], tab * LOG2E, NEG_INF)
    tab = tab.reshape(n_var, N_KV, 4, n_k, n_q).transpose(0, 1, 3, 2, 4)
    return tab.reshape(n_var, N_KV, n_k, 4 * n_q)


def _band_bias(bias_heads, window, n_prev):
    w = (n_prev + 1) * QB
    v = np.arange(n_prev + 1)[:, None, None]
    rel = v * QB + np.arange(QB)[None, :, None] - np.arange(w)[None, None, :]
    return _bias_tiles(bias_heads, rel, (rel >= 0) & (rel < window))


def _slc_bias(bias_heads):
    d = np.arange(3)[:, None, None]
    rel = d * SLC_QB + np.arange(SLC_QB)[None, :, None] - np.arange(SLC_QB)[None, None, :]
    return _bias_tiles(bias_heads, rel, rel >= 0)


def _stack_heads(q_ref, rows, g, lo):
    parts = []
    for c in range(2):
        col = (2 * g + c) * LANES
        q2 = q_ref[rows, col:col + LANES]
        parts += [jnp.where(lo, q2, 0), jnp.where(lo, 0, q2)]
    return jnp.concatenate(parts, axis=0)


def _unstack_heads(o, o_ref, rows, g, lo, n):
    for c in range(2):
        col = (2 * g + c) * LANES
        odd = pltpu.roll(o[(2 * c + 1) * n:(2 * c + 2) * n], HEAD_DIM, axis=1)
        pair = jnp.where(lo, o[2 * c * n:(2 * c + 1) * n], odd)
        o_ref[rows, col:col + LANES] = pair.astype(BF16)


def _with_ones_lane(v, lane):
    return jnp.where(lane < HEAD_DIM, v, jnp.where(lane == HEAD_DIM, 1.0, 0.0).astype(v.dtype))


def _banded_kernel(*refs, n_prev, n_sub, has_sinks):
    if has_sinks:
        q_ref, k_ref, v_ref, bias_ref, sink_ref, o_ref = refs
    else:
        q_ref, k_ref, v_ref, bias_ref, o_ref = refs
    w = (n_prev + 1) * QB
    lane = lax.broadcasted_iota(jnp.int32, (1, LANES), 1)
    lo = lane < HEAD_DIM
    for u in range(n_sub):
        i = (pl.program_id(0) * n_sub + u) % N_QB
        start = pl.multiple_of(jnp.maximum(i - n_prev, 0) * QB, QB)
        var = jnp.minimum(i, n_prev)
        rows = slice(u * QB, (u + 1) * QB)
        for g in range(N_KV):
            kc = k_ref[pl.ds(start, w), g * LANES:(g + 1) * LANES]
            vc = _with_ones_lane(v_ref[pl.ds(start, w), g * LANES:(g + 1) * LANES], lane)
            st = _dot_nt(kc, _stack_heads(q_ref, rows, g, lo)) + bias_ref[var, g]
            m = jnp.max(st, axis=0, keepdims=True)
            if has_sinks:
                sk = sink_ref[g]
                m = jnp.maximum(m, sk)
            ot = _dot_tn(vc, jnp.exp2(st - m).astype(BF16))
            den = ot[HEAD_DIM:HEAD_DIM + 1, :]
            if has_sinks:
                den = den + jnp.exp2(sk - m)
            _unstack_heads((ot / den).T, o_ref, rows, g, lo, QB)


def _banded_attention(q_all, k_all, v_all, bias, sinks, q_col, kv_col, n_prev, n_sub):
    n_tok = q_all.shape[0]
    has_sinks = sinks is not None
    w = (n_prev + 1) * QB
    tq = QB * n_sub
    in_specs = [pl.BlockSpec((tq, 512), lambda gi: (gi, q_col)),
                pl.BlockSpec((SEQ, 256), lambda gi: (gi // (SEQ // tq), kv_col)),
                pl.BlockSpec((SEQ, 256), lambda gi: (gi // (SEQ // tq), kv_col)),
                pl.BlockSpec((n_prev + 1, N_KV, w, 4 * QB), lambda gi: (0, 0, 0, 0))]
    args = [q_all, k_all, v_all, bias]
    if has_sinks:
        in_specs.append(pl.BlockSpec((N_KV, 1, 4 * QB), lambda gi: (0, 0, 0)))
        args.append(sinks)
    return pl.pallas_call(
        functools.partial(_banded_kernel, n_prev=n_prev, n_sub=n_sub, has_sinks=has_sinks),
        grid=(n_tok // tq,),
        in_specs=in_specs,
        out_specs=pl.BlockSpec((tq, 512), lambda gi: (gi, 0)),
        out_shape=jax.ShapeDtypeStruct((n_tok, 512), BF16),
        compiler_params=_cparams(("arbitrary",)),
    )(*args)


def _compress_kernel(z_ref, pos_ref, w1_ref, w2_ref, o_ref):
    half = CMP_STRIDE * HEAD_DIM
    for kv in range(2):
        for g in range(N_KV):
            z = z_ref[0, 2 * kv + g]
            za_h, za_l = _split_bf16(z + pos_ref[kv, :, 0:half])
            zb_h, zb_l = _split_bf16(z + pos_ref[kv, :, half:2 * half])
            w1a_h, w1a_l = w1_ref[kv, 0, 0:half], w1_ref[kv, 1, 0:half]
            w1b_h, w1b_l = w1_ref[kv, 0, half:2 * half], w1_ref[kv, 1, half:2 * half]
            ha = _dot(za_h, w1a_h) + _dot(za_l, w1a_h) + _dot(za_h, w1a_l)
            hb = _dot(zb_h, w1b_h) + _dot(zb_l, w1b_h) + _dot(zb_h, w1b_l)
            h = ha + pltpu.roll(hb, N_CHUNK - 1, axis=0)
            a = jax.nn.gelu(h)
            a_h, a_l = _split_bf16(a)
            out = (_dot(a_h, w2_ref[kv, 0]) + _dot(a_l, w2_ref[kv, 0]) + _dot(a_h, w2_ref[kv, 1]))
            o_ref[0, 2 * kv + g] = out.astype(BF16)


def _compress(c_out, bsz, cmp_pos_k, cmp_w1_k, cmp_w2_k, cmp_pos_v, cmp_w1_v, cmp_w2_v):
    z = c_out.reshape(bsz, N_CHUNK, CMP_STRIDE, 4, HEAD_DIM)
    z = jnp.transpose(z, (0, 3, 1, 2, 4)).reshape(bsz, 4, N_CHUNK, CMP_STRIDE * HEAD_DIM)
    pos = jnp.stack([cmp_pos_k.reshape(1, -1), cmp_pos_v.reshape(1, -1)])

    def split(w):
        hi = w.astype(BF16)
        return jnp.stack([hi, (w - hi.astype(F32)).astype(BF16)])

    w1 = jnp.stack([split(cmp_w1_k), split(cmp_w1_v)])
    w2 = jnp.stack([split(jnp.concatenate([cmp_w2_k, cmp_w2_k], axis=1)),
                    split(jnp.concatenate([cmp_w2_v, cmp_w2_v], axis=1))])
    return pl.pallas_call(
        _compress_kernel,
        grid=(bsz,),
        in_specs=[pl.BlockSpec((1, 4, N_CHUNK, 1024), lambda b: (b, 0, 0, 0)),
                  pl.BlockSpec((2, 1, 2048), lambda b: (0, 0, 0)),
                  pl.BlockSpec((2, 2, 2048, CMP_HIDDEN), lambda b: (0, 0, 0, 0)),
                  pl.BlockSpec((2, 2, CMP_HIDDEN, LANES), lambda b: (0, 0, 0, 0))],
        out_specs=pl.BlockSpec((1, 4, N_CHUNK, LANES), lambda b: (b, 0, 0, 0)),
        out_shape=jax.ShapeDtypeStruct((bsz, 4, N_CHUNK, LANES), BF16),
        compiler_params=_cparams(("arbitrary",)),
    )(z, pos, w1, w2)


CMP_QB = 512


def _cmp_select_kernel(q_ref, kv_ref, ov_ref, o_ref, sel_ref):
    i = pl.program_id(0) % (SEQ // CMP_QB)
    lo = lax.broadcasted_iota(jnp.int32, (1, LANES), 1) < HEAD_DIM
    blk = lax.broadcasted_iota(jnp.int32, (LANES, 1), 0)
    t = i * CMP_QB + lax.broadcasted_iota(jnp.int32, (1, CMP_QB), 1)
    valid = (blk * CMP_STRIDE + (CMP_LEN - 1)) <= t
    valid4 = jnp.concatenate([valid] * 4, axis=1)
    any_valid = jnp.concatenate([t >= CMP_LEN - 1] * 4, axis=1)
    cur = lax.shift_right_logical(t, int(math.log2(SLC_LEN)))
    forced = (blk == 0) | ((blk <= cur) & (blk > cur - SLC_LOCAL))
    future = blk > cur
    blk_f = blk.astype(F32)
    rows = slice(0, CMP_QB)
    for g in range(N_KV):
        kc = kv_ref[0, g]
        vc = kv_ref[0, 2 + g]
        st = jnp.where(valid4, _dot_nt(kc, _stack_heads(q_ref, rows, g, lo)), NEG_INF)
        m = jnp.max(st, axis=0, keepdims=True)
        e = jnp.exp2(st - m)
        p = e / jnp.sum(e, axis=0, keepdims=True)
        p = jnp.where(any_valid, p, 0.0)
        ot = _dot_tn(vc, p.astype(BF16))
        _unstack_heads(ot.T, o_ref, rows, g, lo, CMP_QB)
        psum = (p[:, 0:CMP_QB] + p[:, CMP_QB:2 * CMP_QB]
                + p[:, 2 * CMP_QB:3 * CMP_QB] + p[:, 3 * CMP_QB:4 * CMP_QB])
        p_h, p_l = _split_bf16(psum)
        imp = _dot(ov_ref[...], p_h) + _dot(ov_ref[...], p_l)
        imp = jnp.where(forced, FORCED_SCORE, imp)
        imp = jnp.where(future, NEG_INF, imp)
        imp = jnp.where(blk < N_SLC, imp, -3e38)
        sel = jnp.zeros((LANES, CMP_QB), F32)
        for _ in range(SLC_TOP):
            mx = jnp.max(imp, axis=0, keepdims=True)
            idx = jnp.min(jnp.where(imp == mx, blk_f, 1e9), axis=0, keepdims=True)
            hit = blk_f == idx
            sel = jnp.where(hit, 1.0, sel)
            imp = jnp.where(hit, -3e38, imp)
        sel_ref[g] = sel.astype(BF16)


def _cmp_overlap_matrix():
    nc = (SEQ - CMP_LEN) // CMP_STRIDE + 1
    cs = np.arange(nc)[None, :] * CMP_STRIDE
    ss = np.arange(N_SLC)[:, None] * SLC_LEN
    ov = np.clip(np.minimum(cs + CMP_LEN, ss + SLC_LEN) - np.maximum(cs, ss), 0, None)
    out = np.zeros((LANES, LANES), np.float32)
    out[:N_SLC, :nc] = ov / CMP_LEN
    return jnp.asarray(out, BF16)


def _cmp_select(q_all, kv_cmp):
    n_tok = q_all.shape[0]
    nqb = SEQ // CMP_QB
    return pl.pallas_call(
        _cmp_select_kernel,
        grid=(n_tok // CMP_QB,),
        in_specs=[pl.BlockSpec((CMP_QB, 512), lambda gi: (gi, 1)),
                  pl.BlockSpec((1, 4, N_CHUNK, LANES), lambda gi: (gi // nqb, 0, 0, 0)),
                  pl.BlockSpec((LANES, LANES), lambda gi: (0, 0))],
        out_specs=[pl.BlockSpec((CMP_QB, 512), lambda gi: (gi, 0)),
                   pl.BlockSpec((N_KV, LANES, CMP_QB), lambda gi: (0, 0, gi))],
        out_shape=[jax.ShapeDtypeStruct((n_tok, 512), BF16),
                   jax.ShapeDtypeStruct((N_KV, LANES, n_tok), BF16)],
        compiler_params=_cparams(("arbitrary",)),
    )(q_all, kv_cmp, _cmp_overlap_matrix())


def _selected_kernel(q_ref, k_ref, v_ref, sel_ref, exp_ref, bias_ref, o_ref):
    i = pl.program_id(0) % N_SQB
    lo = lax.broadcasted_iota(jnp.int32, (1, LANES), 1) < HEAD_DIM
    rows = slice(0, SLC_QB)
    qs = [_stack_heads(q_ref, rows, g, lo) for g in range(N_KV)]

    def tile(j, carry):
        ks = pl.multiple_of(j * SLC_QB, SLC_QB)
        d = jnp.minimum(i - j, 2)
        out = []
        for g in range(N_KV):
            m, l, acc = carry[g]
            kt = k_ref[pl.ds(ks, SLC_QB), g * LANES:(g + 1) * LANES]
            vt = v_ref[pl.ds(ks, SLC_QB), g * LANES:(g + 1) * LANES]
            member = _dot(exp_ref[j], sel_ref[g])
            mask_add = (member - 1.0) * (-NEG_INF)
            st = (_dot_nt(kt, qs[g]) + bias_ref[d, g]) + jnp.concatenate([mask_add] * 4, axis=1)
            m_new = jnp.maximum(m, jnp.max(st, axis=0, keepdims=True))
            alpha = jnp.exp2(m - m_new)
            e = jnp.exp2(st - m_new)
            l = alpha * l + jnp.sum(e, axis=0, keepdims=True)
            acc = acc * alpha + _dot_tn(vt, e.astype(BF16))
            out.append((m_new, l, acc))
        return tuple(out)

    init = tuple((jnp.full((1, 4 * SLC_QB), -3e38, F32), jnp.zeros((1, 4 * SLC_QB), F32),
                  jnp.zeros((LANES, 4 * SLC_QB), F32)) for _ in range(N_KV))
    res = lax.fori_loop(0, i + 1, tile, init)
    for g in range(N_KV):
        _, l, acc = res[g]
        _unstack_heads((acc / l).T, o_ref, rows, g, lo, SLC_QB)


def _slc_expand_matrix():
    key_blk = (np.arange(N_SQB)[:, None, None] * SLC_QB + np.arange(SLC_QB)[None, :, None]) // SLC_LEN
    e = (np.arange(LANES)[None, None, :] == key_blk).astype(np.float32)
    return jnp.asarray(e, BF16)


def _selected_attention(q_all, k_all, v_all, sel, bias_heads):
    n_tok = q_all.shape[0]
    return pl.pallas_call(
        _selected_kernel,
        grid=(n_tok // SLC_QB,),
        in_specs=[pl.BlockSpec((SLC_QB, 512), lambda gi: (gi, 1)),
                  pl.BlockSpec((SEQ, 256), lambda gi: (gi // N_SQB, 1)),
                  pl.BlockSpec((SEQ, 256), lambda gi: (gi // N_SQB, 1)),
                  pl.BlockSpec((N_KV, LANES, SLC_QB), lambda gi: (0, 0, gi)),
                  pl.BlockSpec((N_SQB, SLC_QB, LANES), lambda gi: (0, 0, 0)),
                  pl.BlockSpec((3, N_KV, SLC_QB, 4 * SLC_QB), lambda gi: (0, 0, 0, 0))],
        out_specs=pl.BlockSpec((SLC_QB, 512), lambda gi: (gi, 0)),
        out_shape=jax.ShapeDtypeStruct((n_tok, 512), BF16),
        compiler_params=_cparams(("arbitrary",)),
    )(q_all, k_all, v_all, sel, _slc_expand_matrix(), _slc_bias(bias_heads))


MIX_TM = 512
MIX_ROWS = 512


def _layer_norm(x, g, b):
    mu = jnp.mean(x, axis=-1, keepdims=True)
    xc = x - mu
    var = jnp.mean(xc * xc, axis=-1, keepdims=True)
    return xc * lax.rsqrt(var + LN_EPS) * g + b


def _pack_bf16_pair(a, b):
    ia = lax.bitcast_convert_type(a.astype(BF16).astype(F32), jnp.int32)
    ib = lax.bitcast_convert_type(b.astype(BF16).astype(F32), jnp.int32)
    return lax.shift_right_logical(ia, 16) | (ib & -65536)


def _unpack_bf16_pair(w):
    a = lax.bitcast_convert_type(lax.shift_left(w, 16), F32)
    b = lax.bitcast_convert_type(w & -65536, F32)
    return a, b


def _unpack_row(w_ref, k=None):
    parts = []
    for half in range(2):
        parts += list(_unpack_bf16_pair(w_ref[half] if k is None else w_ref[half, k]))
    return jnp.concatenate(parts, axis=1)


def _mix_kernel(x_ref, ya_ref, oc_ref, os_ref, ow_ref, g_ref, m_ref, ge_ref, wa_ref, wb_ref,
                wo_ref, lng_ref, lnb_ref, wr_ref, br_ref, tri_ref,
                h_ref, hw_ref, e_ref, r_ref, g_out_ref, cnt_ref):
    @pl.when(pl.program_id(0) == 0)
    def _():
        cnt_ref[...] = jnp.zeros(cnt_ref.shape, F32)

    for u in range(MIX_TM // MIX_ROWS):
        rs = slice(u * MIX_ROWS, (u + 1) * MIX_ROWS)
        gb = g_ref[rs, :].astype(BF16)
        yb = (_dot(gb, ge_ref[0]).astype(BF16) * oc_ref[rs, :]
              + _dot(gb, ge_ref[1]).astype(BF16) * os_ref[rs, :]
              + _dot(gb, ge_ref[2]).astype(BF16) * ow_ref[rs, :])
        ma = _dot(ya_ref[rs, :], wa_ref[...])
        mb = _dot(yb, wb_ref[...])
        merged = (m_ref[rs, 0:D_MODEL] * ma.astype(BF16)
                  + m_ref[rs, D_MODEL:2 * D_MODEL] * mb.astype(BF16))
        mix = _dot(merged, wo_ref[...])
        h = _layer_norm(DEEPNORM_ALPHA * x_ref[rs, :] + mix, lng_ref[...], lnb_ref[...])
        h_ref[rs, :] = h
        hw_ref[0, rs, :] = _pack_bf16_pair(h[:, 0:256], h[:, 256:512])
        hw_ref[1, rs, :] = _pack_bf16_pair(h[:, 512:768], h[:, 768:1024])
        h_hi, h_lo = _split_bf16(h)
        logits = (_dot_nt(wr_ref[0], h_hi) + _dot_nt(wr_ref[0], h_lo) + _dot_nt(wr_ref[1], h_hi)
                  + br_ref[...])
        row = lax.broadcasted_iota(jnp.int32, logits.shape, 0).astype(F32)
        vals, hits, idxs = [], [], []
        v = logits
        for _ in range(TOP_K):
            mx = jnp.max(v, axis=0, keepdims=True)
            idx = jnp.min(jnp.where(v == mx, row, 1e9), axis=0, keepdims=True)
            hit = row == idx
            vals.append(mx)
            hits.append(hit)
            idxs.append(idx)
            v = jnp.where(hit, -3e38, v)
        es = [jnp.exp(t - vals[0]) for t in vals]
        den = es[0] + es[1] + es[2] + es[3]
        routed = sum(jnp.where(hit, 1.0, 0.0) for hit in hits)
        before = _dot(routed.astype(BF16), tri_ref[...]) + cnt_ref[:, 0:1]
        for k in range(TOP_K):
            e_ref[k:k + 1, rs] = idxs[k].astype(jnp.int32)
            r_ref[k:k + 1, rs] = jnp.sum(jnp.where(hits[k], before, 0.0), axis=0,
                                         keepdims=True).astype(jnp.int32)
            g_out_ref[k:k + 1, rs] = es[k] / den
        cnt_ref[...] = cnt_ref[...] + jnp.sum(routed, axis=1, keepdims=True)


def _gate_expand_matrix():
    e = np.zeros((3, LANES, 512), np.float32)
    for c in range(3):
        for h in range(N_HEADS):
            e[c, 3 * h + c, h * HEAD_DIM:(h + 1) * HEAD_DIM] = 1.0
    return jnp.asarray(e, BF16)


def _mix(x2d, ya, oc, osl, ow, gates, merge, w_branch_a, w_branch_b, w_out, ln_g, ln_b,
         w_router, b_router, part):
    tm = MIX_TM
    n_tok = x2d.shape[0] // MOE_PARTS
    off = part * (n_tok // tm)
    row_in = lambda i: (i + off, 0)
    row = lambda i: (i, 0)
    fix2 = lambda i: (0, 0)
    fix3 = lambda i: (0, 0, 0)
    wr_t = w_router.T
    wr_hi = wr_t.astype(BF16)
    wr = jnp.stack([wr_hi, (wr_t - wr_hi.astype(F32)).astype(BF16)])
    tri = jnp.asarray(np.triu(np.ones((MIX_ROWS, MIX_ROWS), np.float32), k=1), BF16)
    tok = lambda i: (0, i)
    return pl.pallas_call(
        _mix_kernel,
        grid=(n_tok // tm,),
        in_specs=[pl.BlockSpec((tm, D_MODEL), row_in),
                  pl.BlockSpec((tm, 512), row_in), pl.BlockSpec((tm, 512), row_in),
                  pl.BlockSpec((tm, 512), row_in), pl.BlockSpec((tm, 512), row_in),
                  pl.BlockSpec((tm, LANES), row_in), pl.BlockSpec((tm, 2 * D_MODEL), row_in),
                  pl.BlockSpec((3, LANES, 512), fix3),
                  pl.BlockSpec((512, D_MODEL), fix2), pl.BlockSpec((512, D_MODEL), fix2),
                  pl.BlockSpec((D_MODEL, D_MODEL), fix2),
                  pl.BlockSpec((1, D_MODEL), fix2), pl.BlockSpec((1, D_MODEL), fix2),
                  pl.BlockSpec((2, N_EXPERTS, D_MODEL), fix3),
                  pl.BlockSpec((N_EXPERTS, 1), fix2),
                  pl.BlockSpec((MIX_ROWS, MIX_ROWS), fix2)],
        out_specs=[pl.BlockSpec((tm, D_MODEL), row),
                   pl.BlockSpec((2, tm, 256), lambda i: (0, i, 0)),
                   pl.BlockSpec((TOP_K, tm), tok), pl.BlockSpec((TOP_K, tm), tok),
                   pl.BlockSpec((TOP_K, tm), tok),
                   pl.BlockSpec((N_EXPERTS, LANES), fix2)],
        out_shape=[jax.ShapeDtypeStruct((n_tok, D_MODEL), F32),
                   jax.ShapeDtypeStruct((2, n_tok, 256), jnp.int32),
                   jax.ShapeDtypeStruct((TOP_K, n_tok), jnp.int32),
                   jax.ShapeDtypeStruct((TOP_K, n_tok), jnp.int32),
                   jax.ShapeDtypeStruct((TOP_K, n_tok), F32),
                   jax.ShapeDtypeStruct((N_EXPERTS, LANES), F32)],
        compiler_params=_cparams(("arbitrary",)),
    )(x2d, ya, oc, osl, ow, gates, merge, _gate_expand_matrix(),
      w_branch_a.astype(BF16), w_branch_b.astype(BF16), w_out.astype(BF16),
      ln_g[None, :], ln_b[None, :], wr, b_router[:, None], tri)


def _sc_mesh():
    return plsc.VectorSubcoreMesh(core_axis_name="core", subcore_axis_name="subcore")


def _sc_scatter_rows(x, idx, n_out):
    n_tok = x.shape[1]
    n_win = n_tok // SC_WINDOW
    x2 = x.reshape(2 * n_tok, 256)

    @pl.kernel(out_type=jax.ShapeDtypeStruct((n_out, 256), x.dtype), mesh=_sc_mesh(),
               scratch_types=[])
    def scatter(x_hbm, i_hbm, o_hbm):
        def body(x_vmem, i_vmem):
            pltpu.sync_copy(x_vmem, o_hbm.at[i_vmem.at[0]])

        pltpu.emit_pipeline(
            body, grid=(2 * TOP_K * n_win,),
            in_specs=[pl.BlockSpec((SC_WINDOW, 256),
                                   lambda c: ((c // (TOP_K * n_win)) * n_win + c % n_win, 0)),
                      pl.BlockSpec((1, SC_WINDOW), lambda c: (0, c))],
            out_specs=[], core_axis_name=("core", "subcore"),
            dimension_semantics=(pltpu.PARALLEL,))(x_hbm, i_hbm)

    return scatter(x2, idx)


def _sc_gather_rows(y, idx):
    n_idx = idx.shape[1]

    @pl.kernel(out_type=jax.ShapeDtypeStruct((n_idx, 256), y.dtype), mesh=_sc_mesh())
    def gather(y_hbm, i_hbm, o_hbm):
        def body(i_vmem, o_vmem):
            pltpu.sync_copy(y_hbm.at[i_vmem.at[0]], o_vmem)

        pltpu.emit_pipeline(
            body, grid=(n_idx // SC_WINDOW,),
            in_specs=[pl.BlockSpec((1, SC_WINDOW), lambda c: (0, c))],
            out_specs=[pl.BlockSpec((SC_WINDOW, 256), lambda c: (c, 0))],
            core_axis_name=("core", "subcore"),
            dimension_semantics=(pltpu.PARALLEL,))(i_hbm, o_hbm)

    return gather(y, idx)


def _ffn_kernel(be_ref, nu_ref, r_ref, wgu_ref, bgu_ref, wd_ref, bd_ref, y_ref, wgu_s, wd_s):
    i = pl.program_id(0)

    @pl.when(i < nu_ref[0])
    def _():
        @pl.when((i == 0) | (be_ref[i] != be_ref[jnp.maximum(i - 1, 0)]))
        def _():
            wgu_s[...] = wgu_ref[0].astype(BF16)
            wd_s[...] = wd_ref[0].astype(BF16)

        x = _unpack_row(r_ref).astype(BF16)
        hcat = _dot(x, wgu_s[...]) + bgu_ref[0]
        glu = jnp.minimum(hcat[:, 0:D_FF], SWIGLU_LIMIT)
        lin = jnp.clip(hcat[:, D_FF:2 * D_FF], -SWIGLU_LIMIT, SWIGLU_LIMIT)
        act = glu * jax.nn.sigmoid(SWIGLU_ALPHA * glu) * (lin + 1.0)
        y = _dot(act.astype(BF16), wd_s[...]) + bd_ref[0]
        y_ref[0] = _pack_bf16_pair(y[:, 0:256], y[:, 256:512])
        y_ref[1] = _pack_bf16_pair(y[:, 512:768], y[:, 768:1024])


def _expert_ffn(rows, block_expert, n_used, w_gate_up, b_gate_up, w_down, b_down):
    n_slots = rows.shape[1]
    grid_spec = pltpu.PrefetchScalarGridSpec(
        num_scalar_prefetch=2,
        grid=(n_slots // MOE_BLOCK,),
        in_specs=[pl.BlockSpec((2, MOE_BLOCK, 256), lambda i, be, nu: (0, i, 0)),
                  pl.BlockSpec((1, D_MODEL, 2 * D_FF), lambda i, be, nu: (be[i], 0, 0)),
                  pl.BlockSpec((1, 1, 2 * D_FF), lambda i, be, nu: (be[i], 0, 0)),
                  pl.BlockSpec((1, D_FF, D_MODEL), lambda i, be, nu: (be[i], 0, 0)),
                  pl.BlockSpec((1, 1, D_MODEL), lambda i, be, nu: (be[i], 0, 0))],
        out_specs=pl.BlockSpec((2, MOE_BLOCK, 256), lambda i, be, nu: (0, i, 0)),
        scratch_shapes=[pltpu.VMEM((D_MODEL, 2 * D_FF), BF16), pltpu.VMEM((D_FF, D_MODEL), BF16)])
    return pl.pallas_call(
        _ffn_kernel,
        grid_spec=grid_spec,
        out_shape=jax.ShapeDtypeStruct((2, n_slots, 256), jnp.int32),
        compiler_params=_cparams(("arbitrary",)),
    )(block_expert, n_used, rows, w_gate_up, b_gate_up[:, None, :], w_down, b_down[:, None, :])


def _dispatch_plan(eidx, rank, counts, n_tok):
    n_slots = n_tok * TOP_K + N_EXPERTS * MOE_BLOCK
    cnt = counts[:, 0].astype(jnp.int32)
    padded = (cnt + (MOE_BLOCK - 1)) // MOE_BLOCK * MOE_BLOCK
    ends = jnp.cumsum(padded)
    base = ends - padded
    onehot = eidx[..., None] == jnp.arange(N_EXPERTS, dtype=jnp.int32)
    slot = jnp.sum(jnp.where(onehot, base, 0), axis=-1) + rank
    blk_start = jnp.arange(n_slots // MOE_BLOCK, dtype=jnp.int32) * MOE_BLOCK
    block_expert = jnp.sum((ends[None, :] <= blk_start[:, None]).astype(jnp.int32), axis=1)
    block_expert = jnp.minimum(block_expert, N_EXPERTS - 1)
    n_used = (ends[-1:] // MOE_BLOCK).astype(jnp.int32)
    idx = jnp.concatenate([slot, slot + n_slots], axis=0).reshape(1, 2 * TOP_K * n_tok)
    return n_slots, idx, block_expert, n_used


def _moe(hw, eidx, rank, counts, w_gate_up, b_gate_up, w_down, b_down):
    n_tok = hw.shape[1]
    n_slots, idx, block_expert, n_used = _dispatch_plan(eidx, rank, counts, n_tok)
    rows = _sc_scatter_rows(hw, idx, 2 * n_slots).reshape(2, n_slots, 256)
    y = _expert_ffn(rows, block_expert, n_used, w_gate_up, b_gate_up, w_down, b_down)
    yg = _sc_gather_rows(y.reshape(2 * n_slots, 256), idx)
    return yg.reshape(2, TOP_K, n_tok, 256)


def _combine_ln_kernel(h_ref, yg_ref, w_ref, g_ref, b_ref, *rest):
    o_ref = rest[-1]
    ffn = jnp.zeros(h_ref.shape, F32)
    for k in range(TOP_K):
        ffn = ffn + w_ref[:, k:k + 1] * _unpack_row(yg_ref, k)
    o_ref[...] = _layer_norm(DEEPNORM_ALPHA * h_ref[...] + ffn, g_ref[...], b_ref[...])


def _combine_ln(h, yg, w_tok, g, b, part, out_prev):
    n_part = h.shape[0]
    tm = 1024
    off = part * (n_part // tm)
    row = lambda i: (i, 0)
    fix = lambda i: (0, 0)
    in_specs = [pl.BlockSpec((tm, D_MODEL), row),
                pl.BlockSpec((2, TOP_K, tm, 256), lambda i: (0, 0, i, 0)),
                pl.BlockSpec((tm, TOP_K), row),
                pl.BlockSpec((1, D_MODEL), fix), pl.BlockSpec((1, D_MODEL), fix)]
    args = [h, yg, w_tok, g[None, :], b[None, :]]
    aliases = {}
    if out_prev is not None:
        in_specs.append(pl.BlockSpec(memory_space=pl.ANY))
        args.append(out_prev)
        aliases = {len(args) - 1: 0}
    return pl.pallas_call(
        _combine_ln_kernel,
        grid=(n_part // tm,),
        in_specs=in_specs,
        out_specs=pl.BlockSpec((tm, D_MODEL), lambda i: (i + off, 0)),
        out_shape=jax.ShapeDtypeStruct((MOE_PARTS * n_part, D_MODEL), F32),
        input_output_aliases=aliases,
        compiler_params=_cparams(("arbitrary",)),
    )(*args)


def _layer(x2d, bsz, w_in, b_in, rel_bias, attn_sinks, cmp_pos_k, cmp_w1_k, cmp_w2_k, cmp_pos_v,
           cmp_w1_v, cmp_w2_v, w_branch_a, w_branch_b, w_out, ln1_g, ln1_b, w_router, b_router,
           w_gate_up, b_gate_up, w_down, b_down, ln2_g, ln2_b):
    w_packed, b_packed = _pack_in_weights(w_in, b_in)
    q_all, k_all, v_all, gates, merge, c_out = _in_projection(x2d, w_packed, b_packed)

    bias_a = rel_bias[:, :N_HEADS]
    bias_b = rel_bias[:, N_HEADS:]
    sinks = jnp.broadcast_to((attn_sinks * LOG2E).reshape(N_KV, 1, 4, 1), (N_KV, 1, 4, QB))
    sinks = sinks.reshape(N_KV, 1, 4 * QB)
    ya = _banded_attention(q_all, k_all, v_all, _band_bias(bias_a, A_WINDOW, 1), sinks,
                           q_col=0, kv_col=0, n_prev=1, n_sub=8)
    ow = _banded_attention(q_all, k_all, v_all, _band_bias(bias_b, B_WINDOW, 4), None,
                           q_col=1, kv_col=2, n_prev=4, n_sub=4)
    kv_cmp = _compress(c_out, bsz, cmp_pos_k, cmp_w1_k, cmp_w2_k, cmp_pos_v, cmp_w1_v, cmp_w2_v)
    oc, sel = _cmp_select(q_all, kv_cmp)
    osl = _selected_attention(q_all, k_all, v_all, sel, bias_b)

    parts = []
    for p in range(MOE_PARTS):
        h, hw, eidx, rank, gate_w, counts = _mix(x2d, ya, oc, osl, ow, gates, merge, w_branch_a,
                                                 w_branch_b, w_out, ln1_g, ln1_b, w_router,
                                                 b_router, p)
        yg = _moe(hw, eidx, rank, counts, w_gate_up, b_gate_up, w_down, b_down)
        parts.append((h, yg, gate_w.T))
    out = None
    for p, (h, yg, w_tok) in enumerate(parts):
        out = _combine_ln(h, yg, w_tok, ln2_g, ln2_b, p, out)
    return out


def kernel(x, w_in, b_in, rel_bias, attn_sinks, cmp_pos_k, cmp_w1_k, cmp_w2_k, cmp_pos_v, cmp_w1_v,
           cmp_w2_v, w_branch_a, w_branch_b, w_out, ln1_g, ln1_b, w_router, b_router, w_gate_up,
           b_gate_up, w_down, b_down, ln2_g, ln2_b):
    bsz, seq, d = x.shape
    assert seq == SEQ and d == D_MODEL
    h = x.reshape(bsz * seq, d)
    for l in range(w_in.shape[0]):
        h = _layer(h, bsz, w_in[l], b_in[l], rel_bias, attn_sinks[l], cmp_pos_k[l], cmp_w1_k[l],
                   cmp_w2_k[l], cmp_pos_v[l], cmp_w1_v[l], cmp_w2_v[l], w_branch_a[l],
                   w_branch_b[l], w_out[l], ln1_g[l], ln1_b[l], w_router[l], b_router[l],
                   w_gate_up[l], b_gate_up[l], w_down[l], b_down[l], ln2_g[l], ln2_b[l])
    return h.reshape(bsz, seq, d)
```
